```python
import math, functools
import jax, jax.numpy as jnp
from jax import lax
import numpy as np

D_MODEL = 2048
BATCH = 16
SEQ = 2048
DEPTH = 1
DEC_BATCH = 32
DEC_SEQ = 4
PAST_LEN = 16384
PAGE_SIZE = 128

MIX_WIDTH = D_MODEL
ATT_WIDTH = MIX_WIDTH // 2
LRU_WIDTH = MIX_WIDTH - ATT_WIDTH
HEAD_DIM = 128
N_ATT_HEADS = ATT_WIDTH // HEAD_DIM
LRU_BLOCKS = 8
LRU_BLOCK_W = LRU_WIDTH // LRU_BLOCKS
CONV_WIDTH = 4
LRU_C = 8.0
DILATED_PATTERNS = ((128, 1), (512, 4), (2048, 16))
MAX_WINDOW = max(w for w, _ in DILATED_PATTERNS)
N_BUCKETS = 32
MAX_DISTANCE = MAX_WINDOW
D_FF = ((8 * D_MODEL // 3 + 255) // 256) * 256
IN_WIDTH = 3 * ATT_WIDTH + 2 * LRU_WIDTH
RMS_EPS = 1e-6
NEG_INF = -1e30
ATT_SCALE = 1.0 / math.sqrt(HEAD_DIM)

kernel_name = "hybrid_dilated_attn_rglru_macaron_step"


def _rms_norm(x, g):
    xf = x.astype(jnp.float32)
    y = xf * lax.rsqrt(jnp.mean(xf * xf, axis=-1, keepdims=True) + RMS_EPS)
    return (y * g.astype(jnp.float32)).astype(x.dtype)


def _swiglu(h, w_gate, w_up, w_down):
    return (jax.nn.silu(h @ w_gate) * (h @ w_up)) @ w_down


def _rel_bucket(dist):
    max_exact = N_BUCKETS // 2
    df = jnp.maximum(dist, 1).astype(jnp.float32)
    large = max_exact + (jnp.log(df / max_exact) / math.log(MAX_DISTANCE / max_exact)
                         * (N_BUCKETS - max_exact)).astype(jnp.int32)
    return jnp.where(dist < max_exact, dist, jnp.minimum(large, N_BUCKETS - 1))


def _pattern_bias(rel_bias, window, dil):
    dist = dil * jnp.arange(window // dil + 1, dtype=jnp.int32)
    return rel_bias[_rel_bucket(dist)].T.astype(jnp.float32)


def _dilated_prompt(q, k, v, bias_j, window, dil):
    B, S, H, E = q.shape
    n = window // dil
    nb = -(-S // (dil * n))
    Sp = nb * n * dil

    def split(t):
        t = jnp.pad(t, ((0, 0), (0, Sp - S), (0, 0), (0, 0)))
        return t.reshape(B, nb, n, dil, H, E)

    def with_prev(t):
        prev = jnp.concatenate([jnp.zeros_like(t[:, :1]), t[:, :-1]], axis=1)
        return jnp.concatenate([prev, t], axis=2)

    qb = split(q)
    kc = with_prev(split(k))
    vc = with_prev(split(v))
    qq = jnp.arange(n)[:, None]
    kk = jnp.arange(2 * n)[None, :]
    j = n + qq - kk
    band = (j >= 0) & (j <= n)
    not_before_start = (jnp.arange(nb) > 0)[:, None, None] | (kk >= n)[None]
    valid = band[None] & not_before_start
    bias = bias_j[:, jnp.clip(j, 0, n)]
    s = jnp.einsum('bcqrhe,bckrhe->bcrhqk', qb, kc).astype(jnp.float32) * ATT_SCALE
    s = s + bias[None, None, None]
    s = jnp.where(valid[None, :, None, None], s, NEG_INF)
    lse = jax.nn.logsumexp(s, axis=-1)
    p = jnp.exp(s - lse[..., None]).astype(v.dtype)
    o = jnp.einsum('bcrhqk,bckrhe->bcqrhe', p, vc).reshape(B, Sp, H, E)[:, :S]
    lse = lse.transpose(0, 1, 4, 2, 3).reshape(B, Sp, H)[:, :S]
    return o, lse


def _dilated_sample(q, k_all, v_all, bias_j, window, dil):
    B, T = q.shape[:2]
    buf = k_all.shape[1] - T
    n = window // dil
    idx = buf + jnp.arange(T)[:, None] - dil * jnp.arange(n + 1)[None, :]
    valid = idx >= 0
    safe = jnp.maximum(idx, 0)
    kg = k_all[:, safe]
    vg = v_all[:, safe]
    s = jnp.einsum('bthe,btjhe->bhtj', q, kg).astype(jnp.float32) * ATT_SCALE
    s = s + bias_j[:, None, :]
    s = jnp.where(valid[None, None], s, NEG_INF)
    lse = jax.nn.logsumexp(s, axis=-1)
    p = jnp.exp(s - lse[..., None]).astype(v_all.dtype)
    o = jnp.einsum('bhtj,btjhe->bthe', p, vg)
    return o, lse.transpose(0, 2, 1)


def _mix_patterns(outs, lses):
    alpha = jax.nn.softmax(jnp.stack(lses), axis=0)
    o = jnp.einsum('pbth,pbthe->bthe', alpha, jnp.stack(outs).astype(jnp.float32))
    return o.astype(outs[0].dtype)


def _attend_prompt(q, k, v, rel_bias):
    outs, lses = [], []
    for window, dil in DILATED_PATTERNS:
        o, l = _dilated_prompt(q, k, v, _pattern_bias(rel_bias, window, dil), window, dil)
        outs.append(o)
        lses.append(l)
    return _mix_patterns(outs, lses)


def _attend_sample(q, k, v, k_past, v_past, rel_bias):
    k_all = jnp.concatenate([k_past.astype(k.dtype), k], axis=1)
    v_all = jnp.concatenate([v_past.astype(v.dtype), v], axis=1)
    outs, lses = [], []
    for window, dil in DILATED_PATTERNS:
        o, l = _dilated_sample(q, k_all, v_all, _pattern_bias(rel_bias, window, dil), window, dil)
        outs.append(o)
        lses.append(l)
    return _mix_patterns(outs, lses)


def _rg_lru_branch(xl, gate, conv_prefix, h0, conv_w, conv_b, w_a, b_a, w_x, b_x, lam):
    B, T, R = xl.shape
    xc = jnp.concatenate([conv_prefix.astype(xl.dtype), xl], axis=1)
    u = conv_b + sum(conv_w[j] * xc[:, j:j + T] for j in range(CONV_WIDTH))
    ub = u.reshape(B, T, LRU_BLOCKS, LRU_BLOCK_W)
    r = jax.nn.sigmoid(jnp.einsum('btnc,ncd->btnd', ub, w_a).reshape(B, T, R) + b_a)
    i = jax.nn.sigmoid(jnp.einsum('btnc,ncd->btnd', ub, w_x).reshape(B, T, R) + b_x)
    log_a = -LRU_C * r.astype(jnp.float32) * jax.nn.softplus(-lam.astype(jnp.float32))
    a = jnp.exp(log_a)
    bx = jnp.sqrt(-jnp.expm1(2.0 * log_a)) * (i * u).astype(jnp.float32)

    def step(h, ab):
        h = ab[0] * h + ab[1]
        return h, h

    h_last, hs = lax.scan(step, h0.astype(jnp.float32), (a.swapaxes(0, 1), bx.swapaxes(0, 1)))
    y = hs.swapaxes(0, 1).astype(xl.dtype) * jax.nn.gelu(gate)
    return y, xc[:, -(CONV_WIDTH - 1):], h_last


def _layer(x, p, attend, conv_prefix, h0):
    x = x + 0.5 * _swiglu(_rms_norm(x, p['g_ffn1']), p['w1_gate'], p['w1_up'], p['w1_down'])
    h = _rms_norm(x, p['g_mix'])
    B, T, _ = h.shape
    proj = h @ p['w_in']
    q = proj[..., :ATT_WIDTH].reshape(B, T, N_ATT_HEADS, HEAD_DIM)
    k = proj[..., ATT_WIDTH:2 * ATT_WIDTH].reshape(B, T, N_ATT_HEADS, HEAD_DIM)
    v = proj[..., 2 * ATT_WIDTH:3 * ATT_WIDTH].reshape(B, T, N_ATT_HEADS, HEAD_DIM)
    xl = proj[..., 3 * ATT_WIDTH:3 * ATT_WIDTH + LRU_WIDTH]
    gate = proj[..., 3 * ATT_WIDTH + LRU_WIDTH:]
    o_att = attend(q, k, v).reshape(B, T, ATT_WIDTH)
    y_lru, conv_new, h_new = _rg_lru_branch(xl, gate, conv_prefix, h0, p['conv_w'], p['conv_b'],
                                            p['w_a'], p['b_a'], p['w_x'], p['b_x'], p['lam'])
    groups = jnp.concatenate([_rms_norm(o_att, p['g_att_out']), _rms_norm(y_lru, p['g_lru_out'])], axis=-1)
    x = x + groups @ p['w_out']
    x = x + 0.5 * _swiglu(_rms_norm(x, p['g_ffn2']), p['w2_gate'], p['w2_up'], p['w2_down'])
    return x, k, v, conv_new, h_new


def setup_inputs(seed: int = 0) -> dict:
    key = jax.random.key(seed)
    ks = jax.random.split(key, 32)
    f32 = jnp.float32

    def nrm(k, shape, scale):
        return scale * jax.random.normal(k, shape, f32)

    def gain(k, shape):
        return 1.0 + 0.05 * jax.random.normal(k, shape, f32)

    L = DEPTH
    att_buf = min(MAX_WINDOW, PAST_LEN)
    a0 = jax.random.uniform(ks[16], (L, LRU_WIDTH), f32, minval=0.9, maxval=0.999)
    s0 = a0 ** (1.0 / LRU_C)
    lam = jnp.log(s0) - jnp.log1p(-s0)
    return {
        'x_prompt': nrm(ks[0], (BATCH, SEQ, D_MODEL), 1.0),
        'x_sample': nrm(ks[1], (DEC_BATCH, DEC_SEQ, D_MODEL), 1.0),
        'cache_k': nrm(ks[2], (L, DEC_BATCH, att_buf, N_ATT_HEADS, HEAD_DIM), 1.0),
        'cache_v': nrm(ks[3], (L, DEC_BATCH, att_buf, N_ATT_HEADS, HEAD_DIM), 1.0),
        'state_conv': nrm(ks[4], (L, DEC_BATCH, CONV_WIDTH - 1, LRU_WIDTH), 1.0),
        'state_h': nrm(ks[5], (L, DEC_BATCH, LRU_WIDTH), 0.5),
        'g_ffn1': gain(ks[6], (L, D_MODEL)),
        'w1_gate': nrm(ks[7], (L, D_MODEL, D_FF), D_MODEL ** -0.5),
        'w1_up': nrm(ks[8], (L, D_MODEL, D_FF), D_MODEL ** -0.5),
        'w1_down': nrm(ks[9], (L, D_FF, D_MODEL), D_FF ** -0.5),
        'g_mix': gain(ks[10], (L, D_MODEL)),
        'w_in': nrm(ks[11], (L, D_MODEL, IN_WIDTH), D_MODEL ** -0.5),
        'conv_w': nrm(ks[12], (L, CONV_WIDTH, LRU_WIDTH), CONV_WIDTH ** -0.5),
        'conv_b': nrm(ks[13], (L, LRU_WIDTH), 0.01),
        'w_a': nrm(ks[14], (L, LRU_BLOCKS, LRU_BLOCK_W, LRU_BLOCK_W), LRU_BLOCK_W ** -0.5),
        'b_a': nrm(ks[15], (L, LRU_WIDTH), 0.01),
        'w_x': nrm(ks[17], (L, LRU_BLOCKS, LRU_BLOCK_W, LRU_BLOCK_W), LRU_BLOCK_W ** -0.5),
        'b_x': nrm(ks[18], (L, LRU_WIDTH), 0.01),
        'lam': lam,
        'rel_bias': nrm(ks[19], (N_BUCKETS, N_ATT_HEADS), 0.3),
        'g_att_out': gain(ks[20], (L, ATT_WIDTH)),
        'g_lru_out': gain(ks[21], (L, LRU_WIDTH)),
        'w_out': nrm(ks[22], (L, MIX_WIDTH, D_MODEL), MIX_WIDTH ** -0.5),
        'g_ffn2': gain(ks[23], (L, D_MODEL)),
        'w2_gate': nrm(ks[24], (L, D_MODEL, D_FF), D_MODEL ** -0.5),
        'w2_up': nrm(ks[25], (L, D_MODEL, D_FF), D_MODEL ** -0.5),
        'w2_down': nrm(ks[26], (L, D_FF, D_MODEL), D_FF ** -0.5),
        'g_final': gain(ks[27], (D_MODEL,)),
    }


def reference(x_prompt, x_sample, cache_k, cache_v, state_conv, state_h,
              g_ffn1, w1_gate, w1_up, w1_down, g_mix, w_in, conv_w, conv_b,
              w_a, b_a, w_x, b_x, lam, rel_bias, g_att_out, g_lru_out, w_out,
              g_ffn2, w2_gate, w2_up, w2_down, g_final):
    keep = min(MAX_WINDOW, x_prompt.shape[1])
    xp, xs = x_prompt, x_sample
    kp, vp, cp, hp = [], [], [], []
    ksm, vsm, csm, hsm = [], [], [], []
    for l in range(DEPTH):
        p = {'g_ffn1': g_ffn1[l], 'w1_gate': w1_gate[l], 'w1_up': w1_up[l], 'w1_down': w1_down[l],
             'g_mix': g_mix[l], 'w_in': w_in[l], 'conv_w': conv_w[l], 'conv_b': conv_b[l],
             'w_a': w_a[l], 'b_a': b_a[l], 'w_x': w_x[l], 'b_x': b_x[l], 'lam': lam[l],
             'g_att_out': g_att_out[l], 'g_lru_out': g_lru_out[l], 'w_out': w_out[l],
             'g_ffn2': g_ffn2[l], 'w2_gate': w2_gate[l], 'w2_up': w2_up[l], 'w2_down': w2_down[l]}
        zero_conv = jnp.zeros((xp.shape[0], CONV_WIDTH - 1, LRU_WIDTH), xp.dtype)
        zero_h = jnp.zeros((xp.shape[0], LRU_WIDTH), jnp.float32)
        xp, k_l, v_l, c_l, h_l = _layer(xp, p, functools.partial(_attend_prompt, rel_bias=rel_bias),
                                        zero_conv, zero_h)
        kp.append(k_l[:, -keep:])
        vp.append(v_l[:, -keep:])
        cp.append(c_l)
        hp.append(h_l)
        attend_s = functools.partial(_attend_sample, k_past=cache_k[l], v_past=cache_v[l], rel_bias=rel_bias)
        xs, k_l, v_l, c_l, h_l = _layer(xs, p, attend_s, state_conv[l], state_h[l])
        ksm.append(k_l)
        vsm.append(v_l)
        csm.append(c_l)
        hsm.append(h_l)
    y_prompt = _rms_norm(xp, g_final)
    y_sample = _rms_norm(xs, g_final)
    return (y_prompt, y_sample, jnp.stack(kp), jnp.stack(vp), jnp.stack(cp), jnp.stack(hp),
            jnp.stack(ksm), jnp.stack(vsm), jnp.stack(csm), jnp.stack(hsm))
```

```python
import functools
import math

import jax
import jax.numpy as jnp
import numpy as np
from jax import lax
from jax.experimental import pallas as pl
from jax.experimental.pallas import tpu as pltpu

F32 = jnp.float32
BF16 = jnp.bfloat16

HEAD_DIM = 128
LRU_BLOCK_W = 128
CONV_WIDTH = 4
LRU_C = 8.0
DILATED_PATTERNS = ((128, 1), (512, 4), (2048, 16))
MAX_WINDOW = 2048
N_BUCKETS = 32
MAX_DISTANCE = MAX_WINDOW
RMS_EPS = 1e-6
NEG_INF = -1e30
ATT_SCALE = 1.0 / math.sqrt(HEAD_DIM)

LANE = 128
MIB = 1024 * 1024
VMEM_LIMIT = 56 * MIB


def _cparams(semantics):
    return pltpu.CompilerParams(dimension_semantics=semantics, vmem_limit_bytes=VMEM_LIMIT)


def _rms(x, g):
    ms = jnp.mean(x * x, axis=-1, keepdims=True)
    return (x * lax.rsqrt(ms + RMS_EPS)) * g


def _sigmoid(x):
    return 1.0 / (1.0 + jnp.exp(-x))


def _gelu_tanh(x):
    c = math.sqrt(2.0 / math.pi)
    return x * (0.5 * (1.0 + jnp.tanh(c * (x + 0.044715 * (x * x * x)))))


def _softplus(z):
    return jnp.maximum(z, 0.0) + jnp.log1p(jnp.exp(-jnp.abs(z)))


def _neg_expm1_2x(x):
    t = jnp.tanh(x)
    return (-2.0 * t) / (1.0 - t)


def _ffn_kernel(x_ref, g_ref, wg_ref, wu_ref, wd_ref, g2_ref, o_ref, *rest, nj, emit_x):
    if emit_x:
        h2_ref, hn_ref = rest
    else:
        (hn_ref,) = rest
    j = pl.program_id(1)

    @pl.when(j == 0)
    def _():
        hn_ref[...] = _rms(x_ref[...], g_ref[...]).astype(BF16)

    hn = hn_ref[...]
    a = jnp.dot(hn, wg_ref[...], preferred_element_type=F32)
    b = jnp.dot(hn, wu_ref[...], preferred_element_type=F32)
    act = ((a * _sigmoid(a)) * b).astype(BF16)
    d = jnp.dot(act, wd_ref[...], preferred_element_type=F32)

    @pl.when(j == 0)
    def _():
        o_ref[...] = d

    @pl.when(j > 0)
    def _():
        o_ref[...] += d

    @pl.when(j == nj - 1)
    def _():
        xn = x_ref[...] + 0.5 * o_ref[...]
        if emit_x:
            o_ref[...] = xn
            h2_ref[...] = _rms(xn, g2_ref[...]).astype(BF16)
        else:
            o_ref[...] = _rms(xn, g2_ref[...])


def _ffn(x, g, wg, wu, wd, g2, *, tm, tf, emit_x):
    M, D = x.shape
    Fd = wg.shape[1]
    nj = Fd // tf
    out_shape = [jax.ShapeDtypeStruct((M, D), F32)]
    out_specs = [pl.BlockSpec((tm, D), lambda i, j: (i, 0))]
    if emit_x:
        out_shape.append(jax.ShapeDtypeStruct((M, D), BF16))
        out_specs.append(pl.BlockSpec((tm, D), lambda i, j: (i, 0)))
    res = pl.pallas_call(
        functools.partial(_ffn_kernel, nj=nj, emit_x=emit_x),
        grid=(M // tm, nj),
        in_specs=[
            pl.BlockSpec((tm, D), lambda i, j: (i, 0)),
            pl.BlockSpec((1, D), lambda i, j: (0, 0)),
            pl.BlockSpec((D, tf), lambda i, j: (0, j)),
            pl.BlockSpec((D, tf), lambda i, j: (0, j)),
            pl.BlockSpec((tf, D), lambda i, j: (j, 0)),
            pl.BlockSpec((1, D), lambda i, j: (0, 0)),
        ],
        out_specs=out_specs,
        out_shape=out_shape,
        scratch_shapes=[pltpu.VMEM((tm, D), BF16)],
        compiler_params=_cparams(("parallel", "arbitrary")),
        name="ffn_x" if emit_x else "ffn_final",
    )(x, g, wg, wu, wd, g2)
    return res if emit_x else res[0]


def _inproj_kernel(h_ref, w_ref, q_ref, k_ref, v_ref, kb_ref, vb_ref, xl_ref, gt_ref):
    j = pl.program_id(1)
    r = jnp.dot(h_ref[...], w_ref[...], preferred_element_type=F32)

    @pl.when(j == 0)
    def _():
        q_ref[...] = (r * ATT_SCALE).astype(BF16)

    @pl.when(j == 1)
    def _():
        k_ref[...] = r
        kb_ref[...] = r.astype(BF16)

    @pl.when(j == 2)
    def _():
        v_ref[...] = r
        vb_ref[...] = r.astype(BF16)

    @pl.when(j == 3)
    def _():
        xl_ref[...] = r

    @pl.when(j == 4)
    def _():
        gt_ref[...] = r


def _inproj(h, w_in, *, tm, width):
    M, D = h.shape
    assert w_in.shape[1] == 5 * width
    ospec = pl.BlockSpec((tm, width), lambda i, j: (i, 0))
    dts = [BF16, F32, F32, BF16, BF16, F32, F32]
    return pl.pallas_call(
        _inproj_kernel,
        grid=(M // tm, 5),
        in_specs=[
            pl.BlockSpec((tm, D), lambda i, j: (i, 0)),
            pl.BlockSpec((D, width), lambda i, j: (0, j)),
        ],
        out_specs=[ospec] * 7,
        out_shape=[jax.ShapeDtypeStruct((M, width), dt) for dt in dts],
        compiler_params=_cparams(("parallel", "arbitrary")),
        name="inproj",
    )(h, w_in)


def _rel_bucket(dist):
    max_exact = N_BUCKETS // 2
    df = jnp.maximum(dist, 1).astype(F32)
    large = max_exact + (jnp.log(df / max_exact) / math.log(MAX_DISTANCE / max_exact)
                         * (N_BUCKETS - max_exact)).astype(jnp.int32)
    return jnp.where(dist < max_exact, dist, jnp.minimum(large, N_BUCKETS - 1))


def _band_bias_t(rel_bias, window, dil):
    n = window // dil
    assert n == LANE
    bias_j = rel_bias[_rel_bucket(dil * jnp.arange(n + 1, dtype=jnp.int32))].T.astype(F32)
    kk = np.arange(2 * n)[:, None]
    qq = np.arange(n)[None, :]
    j = n + qq - kk
    valid = (j >= 0) & (j <= n)
    t = bias_j[:, np.clip(j, 0, n)]
    return jnp.where(jnp.asarray(valid)[None], t, NEG_INF)


def _attn_kernel(q_ref, k_ref, v_ref, bias_ref, o_ref, lse_ref, *, W, nblk, H):
    g = pl.program_id(1)
    nt = (((1,), (1,)), ((), ()))
    tn = (((0,), (0,)), ((), ()))
    for w in range(W):
        h = (g * W + w) % H
        cs = slice(w * LANE, (w + 1) * LANE)
        for c in range(nblk):
            q = q_ref[0, c * LANE:(c + 1) * LANE, cs]
            if c == 0:
                k2 = k_ref[0, 0:LANE, cs]
                v2 = v_ref[0, 0:LANE, cs]
                bias = bias_ref[h, LANE:2 * LANE, :]
            else:
                k2 = k_ref[0, (c - 1) * LANE:(c + 1) * LANE, cs]
                v2 = v_ref[0, (c - 1) * LANE:(c + 1) * LANE, cs]
                bias = bias_ref[h]
            st = lax.dot_general(k2, q, nt, preferred_element_type=F32) + bias
            m = jnp.max(st, axis=0, keepdims=True)
            p = jnp.exp(st - m)
            l = jnp.sum(p, axis=0, keepdims=True)
            pn = (p * (1.0 / l)).astype(BF16)
            o = lax.dot_general(pn, v2, tn, preferred_element_type=F32)
            o_ref[0, c * LANE:(c + 1) * LANE, cs] = o.astype(o_ref.dtype)
            lse_ref[0, 0, w:w + 1, c * LANE:(c + 1) * LANE] = m + jnp.log(l)


def _attn_pattern(q, k, v, bias_t, *, B, S, H, dil):
    R = S // dil
    C = dil * H
    nblk = R // LANE
    W = max(1, 16 // nblk)
    W = min(W, C)
    width = H * HEAD_DIM
    qv = q.reshape(B, R, dil * width)
    kv = k.reshape(B, R, dil * width)
    vv = v.reshape(B, R, dil * width)
    spec = pl.BlockSpec((1, R, W * LANE), lambda b, g: (b, 0, g))
    o, lse = pl.pallas_call(
        functools.partial(_attn_kernel, W=W, nblk=nblk, H=H),
        grid=(B, C // W),
        in_specs=[spec, spec, spec,
                  pl.BlockSpec((H, 2 * LANE, LANE), lambda b, g: (0, 0, 0))],
        out_specs=[spec, pl.BlockSpec((1, 1, W, R), lambda b, g: (b, g, 0, 0))],
        out_shape=[jax.ShapeDtypeStruct((B, R, dil * width), BF16),
                   jax.ShapeDtypeStruct((B, C // W, W, R), F32)],
        compiler_params=_cparams(("parallel", "parallel")),
        name=f"attn_d{dil}",
    )(qv, kv, vv, bias_t)
    o = o.reshape(B * S, width)
    lse = lse.reshape(B, dil, H, R).transpose(0, 3, 1, 2).reshape(B * S, H)
    return o, lse


def _lru_gate_block(un, n, wax_ref, ba_ref, bx_ref, sp):
    sl = slice(n * LRU_BLOCK_W, (n + 1) * LRU_BLOCK_W)
    gx = jnp.dot(un.astype(BF16), wax_ref[n], preferred_element_type=F32)
    r = _sigmoid(gx[:, :LRU_BLOCK_W] + ba_ref[:, sl])
    i = _sigmoid(gx[:, LRU_BLOCK_W:] + bx_ref[:, sl])
    log_a = (-LRU_C * r) * sp[:, sl]
    return jnp.exp(log_a), jnp.sqrt(_neg_expm1_2x(log_a)) * (i * un)


def _lru_kernel(xl_ref, gt_ref, cw_ref, cb_ref, wax_ref, ba_ref, bx_ref, lam_ref, g_ref,
                yn_ref, hl_ref, xc_s, a_s, b_s, h_s, *, B, tT):
    t = pl.program_id(0)
    R = xl_ref.shape[-1]
    NB = R // LRU_BLOCK_W
    pad = 8

    @pl.when(t == 0)
    def _():
        xc_s[:, 0:pad, :] = jnp.zeros((B, pad, R), F32)
        h_s[...] = jnp.zeros_like(h_s)

    xc_s[:, pad:, :] = xl_ref[...]
    u = cb_ref[...][None]
    for j in range(CONV_WIDTH):
        off = pad - (CONV_WIDTH - 1) + j
        u = u + cw_ref[j:j + 1, :][None] * xc_s[:, off:off + tT, :]
    xc_s[:, pad - (CONV_WIDTH - 1):pad, :] = xc_s[:, pad + tT - (CONV_WIDTH - 1):pad + tT, :]

    sp = _softplus(-lam_ref[...])
    u2 = u.reshape(B * tT, R)
    for n in range(NB):
        a, bx = _lru_gate_block(u2[:, n * LRU_BLOCK_W:(n + 1) * LRU_BLOCK_W], n, wax_ref, ba_ref, bx_ref, sp)
        a_s[n] = a
        b_s[n] = bx

    def step(tt, hs):
        rows = pl.ds(tt, B, stride=tT)
        new = []
        for n in range(NB):
            hn = a_s[n, rows, :] * hs[n] + b_s[n, rows, :]
            b_s[n, rows, :] = hn
            new.append(hn)
        return tuple(new)

    hs = lax.fori_loop(0, tT, step, tuple(h_s[n] for n in range(NB)))
    for n in range(NB):
        h_s[n] = hs[n]
        hl_ref[:, n * LRU_BLOCK_W:(n + 1) * LRU_BLOCK_W] = hs[n]

    hseq = jnp.concatenate([b_s[n] for n in range(NB)], axis=1)
    y = hseq * _gelu_tanh(gt_ref[...].reshape(B * tT, R))
    yn_ref[...] = _rms(y, g_ref[...]).astype(BF16).reshape(B, tT, R)


def _lru_prompt(xl, gate, cw, cb, wax, ba, bx, lam, g, *, B, T, tT):
    R = xl.shape[-1]
    xl3 = xl.reshape(B, T, R)
    gt3 = gate.reshape(B, T, R)
    row = lambda shp: pl.BlockSpec(shp, lambda t: (0,) * len(shp))
    blk = pl.BlockSpec((B, tT, R), lambda t: (0, t, 0))
    yn, hl = pl.pallas_call(
        functools.partial(_lru_kernel, B=B, tT=tT),
        grid=(T // tT,),
        in_specs=[blk, blk, row(cw.shape), row(cb.shape), row(wax.shape), row(ba.shape),
                  row(bx.shape), row(lam.shape), row(g.shape)],
        out_specs=[blk, row((B, R))],
        out_shape=[jax.ShapeDtypeStruct((B, T, R), BF16), jax.ShapeDtypeStruct((B, R), F32)],
        scratch_shapes=[pltpu.VMEM((B, tT + 8, R), F32),
                        pltpu.VMEM((R // LRU_BLOCK_W, B * tT, LRU_BLOCK_W), F32),
                        pltpu.VMEM((R // LRU_BLOCK_W, B * tT, LRU_BLOCK_W), F32),
                        pltpu.VMEM((R // LRU_BLOCK_W, B, LRU_BLOCK_W), F32)],
        compiler_params=_cparams(("arbitrary",)),
        name="lru_prompt",
    )(xl3, gt3, cw, cb, wax, ba, bx, lam, g)
    return yn.reshape(B * T, R), hl


def _lru_sample_kernel(xc_ref, gt_ref, h0_ref, cw_ref, cb_ref, wax_ref, ba_ref, bx_ref, lam_ref,
                       g_ref, yn_ref, hl_ref, *, B, T):
    NB = xc_ref.shape[0]
    L = CONV_WIDTH - 1 + T
    sp = _softplus(-lam_ref[...])
    hs = [h0_ref[:, n * LRU_BLOCK_W:(n + 1) * LRU_BLOCK_W] for n in range(NB)]
    for t in range(T):
        ys = []
        for n in range(NB):
            sl = slice(n * LRU_BLOCK_W, (n + 1) * LRU_BLOCK_W)
            un = cb_ref[:, sl]
            for j in range(CONV_WIDTH):
                un = un + cw_ref[j:j + 1, sl] * xc_ref[n, pl.ds(t + j, B, stride=L), :]
            a, bx = _lru_gate_block(un, n, wax_ref, ba_ref, bx_ref, sp)
            hs[n] = a * hs[n] + bx
            ys.append(hs[n] * _gelu_tanh(gt_ref[n, pl.ds(t, B, stride=T), :]))
        yn = _rms(jnp.concatenate(ys, axis=1), g_ref[...])
        for n in range(NB):
            yn_ref[n, pl.ds(t, B, stride=T), :] = yn[:, n * LRU_BLOCK_W:(n + 1) * LRU_BLOCK_W]
    for n in range(NB):
        hl_ref[:, n * LRU_BLOCK_W:(n + 1) * LRU_BLOCK_W] = hs[n]


def _lru_sample(xc, gate, h0, cw, cb, wax, ba, bx, lam, g, *, B, T):
    R = gate.shape[-1]
    NB = R // LRU_BLOCK_W
    split = lambda a: a.reshape(-1, NB, LRU_BLOCK_W).transpose(1, 0, 2)
    yn, hl = pl.pallas_call(
        functools.partial(_lru_sample_kernel, B=B, T=T),
        out_shape=[jax.ShapeDtypeStruct((NB, B * T, LRU_BLOCK_W), F32),
                   jax.ShapeDtypeStruct((B, R), F32)],
        compiler_params=pltpu.CompilerParams(vmem_limit_bytes=VMEM_LIMIT),
        name="lru_sample",
    )(split(xc), split(gate), h0, cw, cb, wax, ba, bx, lam, g)
    return yn.transpose(1, 0, 2).reshape(B * T, R), hl


def _attn_sample_kernel(qt_ref, kn_ref, vn_ref, ck_ref, cv_ref, bias_ref, mult_ref, o_ref,
                        kb_s, vb_s, *, H, T, buf):
    width = H * HEAD_DIM
    npad = kb_s.shape[0] - buf - kn_ref.shape[1]
    kb_s[0:buf, :] = ck_ref[0].astype(BF16)
    vb_s[0:buf, :] = cv_ref[0].astype(BF16)
    kb_s[buf:buf + kn_ref.shape[1], :] = kn_ref[0]
    vb_s[buf:buf + vn_ref.shape[1], :] = vn_ref[0]
    kb_s[buf + kn_ref.shape[1]:, :] = jnp.zeros((npad, width), BF16)
    vb_s[buf + vn_ref.shape[1]:, :] = jnp.zeros((npad, width), BF16)

    rows = H * T
    row_head = lax.broadcasted_iota(jnp.int32, (rows, width), 0) // T
    lane_head = lax.broadcasted_iota(jnp.int32, (rows, width), 1) // HEAD_DIM
    qbd = jnp.where(row_head == lane_head, qt_ref[0].astype(F32), 0.0).astype(BF16)
    s = lax.dot_general(qbd, kb_s[...], (((1,), (1,)), ((), ())), preferred_element_type=F32)
    s = s + bias_ref[...]
    m = jnp.max(s, axis=-1, keepdims=True)
    p = jnp.exp(s - m) * mult_ref[...]
    l = jnp.sum(p, axis=-1, keepdims=True)
    pn = (p * (1.0 / l)).astype(BF16)
    o = jnp.dot(pn, vb_s[...], preferred_element_type=F32)
    for h in range(H):
        o_ref[0, :, h * HEAD_DIM:(h + 1) * HEAD_DIM] = o[h * T:(h + 1) * T, h * HEAD_DIM:(h + 1) * HEAD_DIM]


def _sample_tables(rel_bias, *, H, T, buf, ktot):
    slot = np.arange(ktot)[None, :]
    tq = np.arange(T)[:, None]
    dist = np.where(slot < buf, buf + tq - slot, tq - (slot - buf))
    real = (slot < buf + T) & (dist >= 0)
    mult = np.zeros((T, ktot), np.float32)
    for window, dil in DILATED_PATTERNS:
        mult += (real & (dist % dil == 0) & (dist // dil <= window // dil)).astype(np.float32)
    bucket = _rel_bucket(jnp.asarray(np.maximum(dist, 0), dtype=jnp.int32))
    bias = rel_bias[bucket].astype(F32)
    bias = jnp.where(jnp.asarray(mult > 0)[:, :, None], bias, NEG_INF)
    bias = bias.transpose(2, 0, 1).reshape(H * T, ktot)
    mult = np.tile(mult[None], (H, 1, 1)).reshape(H * T, ktot)
    return bias, jnp.asarray(mult)


def _attn_sample(q, kb, vb, cache_k, cache_v, rel_bias, *, B, T, H):
    width = H * HEAD_DIM
    buf = cache_k.shape[1]
    tpad = 16
    ktot = buf + LANE
    qt = jnp.tile(q.reshape(B, T, width), (1, H, 1))
    pad = lambda a: jnp.pad(a.reshape(B, T, width), ((0, 0), (0, tpad - T), (0, 0)))
    bias, mult = _sample_tables(rel_bias, H=H, T=T, buf=buf, ktot=ktot)
    full = lambda shp: pl.BlockSpec(shp, lambda b: (0,) * len(shp))
    per_b = lambda r: pl.BlockSpec((1, r, width), lambda b: (b, 0, 0))
    return pl.pallas_call(
        functools.partial(_attn_sample_kernel, H=H, T=T, buf=buf),
        grid=(B,),
        in_specs=[per_b(H * T), per_b(tpad), per_b(tpad), per_b(buf), per_b(buf),
                  full((H * T, ktot)), full((H * T, ktot))],
        out_specs=per_b(T),
        out_shape=jax.ShapeDtypeStruct((B, T, width), F32),
        scratch_shapes=[pltpu.VMEM((ktot, width), BF16), pltpu.VMEM((ktot, width), BF16)],
        compiler_params=_cparams(("parallel",)),
        name="attn_sample",
    )(qt, pad(kb), pad(vb), cache_k, cache_v, bias, mult)


def _outproj_kernel(*refs, npat, H):
    x_ref = refs[0]
    o_refs = refs[1:1 + npat]
    k = 1 + npat
    if npat > 1:
        lse_ref = refs[k]
        k += 1
    yn_ref, g_ref, w_ref, out_ref = refs[k:k + 4]
    aw = w_ref.shape[0] // 2
    if npat > 1:
        ls = [lse_ref[p] for p in range(npat)]
        m = functools.reduce(jnp.maximum, ls)
        es = [jnp.exp(l - m) for l in ls]
        inv = 1.0 / functools.reduce(lambda a, b: a + b, es)
        al = [e * inv for e in es]
        parts = []
        for h in range(H):
            sl = slice(h * HEAD_DIM, (h + 1) * HEAD_DIM)
            acc = al[0][:, h:h + 1] * o_refs[0][:, sl].astype(F32)
            for p in range(1, npat):
                acc = acc + al[p][:, h:h + 1] * o_refs[p][:, sl].astype(F32)
            parts.append(acc)
        oatt = jnp.concatenate(parts, axis=1)
    else:
        oatt = o_refs[0][...].astype(F32)
    an = _rms(oatt, g_ref[...]).astype(BF16)
    acc = jnp.dot(an, w_ref[0:aw, :], preferred_element_type=F32)
    acc = acc + jnp.dot(yn_ref[...].astype(BF16), w_ref[aw:, :], preferred_element_type=F32)
    out_ref[...] = x_ref[...] + acc


def _outproj(x, outs, lse, yn, g_att, w_out, *, tm, H):
    M, D = x.shape
    npat = len(outs)
    aw = outs[0].shape[1]
    tok = lambda w: pl.BlockSpec((tm, w), lambda i: (i, 0))
    in_specs = [tok(D)] + [tok(aw)] * npat
    args = [x] + list(outs)
    if npat > 1:
        in_specs.append(pl.BlockSpec((npat, tm, H), lambda i: (0, i, 0)))
        args.append(lse)
    in_specs += [tok(yn.shape[1]), pl.BlockSpec((1, aw), lambda i: (0, 0)),
                 pl.BlockSpec(w_out.shape, lambda i: (0, 0))]
    args += [yn, g_att, w_out]
    return pl.pallas_call(
        functools.partial(_outproj_kernel, npat=npat, H=H),
        grid=(M // tm,),
        in_specs=in_specs,
        out_specs=tok(D),
        out_shape=jax.ShapeDtypeStruct((M, D), F32),
        compiler_params=_cparams(("parallel",)),
        name=f"outproj{npat}",
    )(*args)


def kernel(x_prompt, x_sample, cache_k, cache_v, state_conv, state_h, g_ffn1, w1_gate, w1_up, w1_down, g_mix, w_in, conv_w, conv_b, w_a, b_a, w_x, b_x, lam, rel_bias, g_att_out, g_lru_out, w_out, g_ffn2, w2_gate, w2_up, w2_down, g_final):
    B, S, D = x_prompt.shape
    Bs, Ts, _ = x_sample.shape
    depth = g_ffn1.shape[0]
    assert depth == 1
    R = conv_w.shape[-1]
    H = rel_bias.shape[1]
    width = H * HEAD_DIM
    buf = cache_k.shape[2]
    l = 0

    row = lambda a: a.reshape(1, -1)
    bf = lambda a: a.astype(BF16)
    wg1, wu1, wd1 = bf(w1_gate[l]), bf(w1_up[l]), bf(w1_down[l])
    wg2, wu2, wd2 = bf(w2_gate[l]), bf(w2_up[l]), bf(w2_down[l])
    win, wout = bf(w_in[l]), bf(w_out[l])
    wax = bf(jnp.concatenate([w_a[l], w_x[l]], axis=-1))
    lru_w = (conv_w[l], row(conv_b[l]), wax, row(b_a[l]), row(b_x[l]), row(lam[l]), row(g_lru_out[l]))
    tf = 512

    def trunk_in(x2d, tm):
        x1, hmix = _ffn(x2d, row(g_ffn1[l]), wg1, wu1, wd1, row(g_mix[l]), tm=tm, tf=tf, emit_x=True)
        return (x1,) + tuple(_inproj(hmix, win, tm=tm, width=width))

    def trunk_out(x1, outs, lse, yn, tm):
        x2 = _outproj(x1, outs, lse, yn, row(g_att_out[l]), wout, tm=tm, H=H)
        return _ffn(x2, row(g_ffn2[l]), wg2, wu2, wd2, row(g_final), tm=tm, tf=tf, emit_x=False)

    Mp = B * S
    x1, q, k, v, kb, vb, xl, gate = trunk_in(x_prompt.reshape(Mp, D), 512)
    outs, lses = [], []
    for window, dil in DILATED_PATTERNS:
        o, lse = _attn_pattern(q, kb, vb, _band_bias_t(rel_bias, window, dil), B=B, S=S, H=H, dil=dil)
        outs.append(o)
        lses.append(lse)
    yn, h_p = _lru_prompt(xl, gate, *lru_w, B=B, T=S, tT=64)
    y_prompt = trunk_out(x1, outs, jnp.stack(lses), yn, 512).reshape(B, S, D)
    keep = min(MAX_WINDOW, S)
    k_prompt = k.reshape(B, S, H, HEAD_DIM)[:, S - keep:][None]
    v_prompt = v.reshape(B, S, H, HEAD_DIM)[:, S - keep:][None]
    conv_prompt = xl.reshape(B, S, R)[:, S - (CONV_WIDTH - 1):][None]

    Ms = Bs * Ts
    x1s, qs, ks, vs, kbs, vbs, xls, gates = trunk_in(x_sample.reshape(Ms, D), Ms)
    o_s = _attn_sample(qs, kbs, vbs, cache_k[l].reshape(Bs, buf, width), cache_v[l].reshape(Bs, buf, width),
                       rel_bias, B=Bs, T=Ts, H=H)
    xc = jnp.concatenate([state_conv[l], xls.reshape(Bs, Ts, R)], axis=1)
    yns, h_s = _lru_sample(xc, gates, state_h[l], *lru_w, B=Bs, T=Ts)
    y_sample = trunk_out(x1s, [o_s.reshape(Ms, width)], None, yns, Ms).reshape(Bs, Ts, D)
    k_sample = ks.reshape(Bs, Ts, H, HEAD_DIM)[None]
    v_sample = vs.reshape(Bs, Ts, H, HEAD_DIM)[None]
    conv_sample = xc[:, -(CONV_WIDTH - 1):][None]

    return (y_prompt, y_sample, k_prompt, v_prompt, conv_prompt, h_p[None],
            k_sample, v_sample, conv_sample, h_s[None])
```

```python
import functools
import math

import jax
import jax.numpy as jnp
import numpy as np
from jax import lax
from jax.experimental import pallas as pl
from jax.experimental.pallas import tpu as pltpu

F32 = jnp.float32
BF16 = jnp.bfloat16

HEAD_DIM = 128
LRU_BLOCK_W = 128
CONV_WIDTH = 4
LRU_C = 8.0
DILATED_PATTERNS = ((128, 1), (512, 4), (2048, 16))
MAX_WINDOW = 2048
N_BUCKETS = 32
MAX_DISTANCE = MAX_WINDOW
RMS_EPS = 1e-6
NEG_INF = -1e30
ATT_SCALE = 1.0 / math.sqrt(HEAD_DIM)

LANE = 128
MIB = 1024 * 1024
VMEM_LIMIT = 56 * MIB


def _cparams(semantics):
    return pltpu.CompilerParams(dimension_semantics=semantics, vmem_limit_bytes=VMEM_LIMIT)


def _rms(x, g):
    ms = jnp.mean(x * x, axis=-1, keepdims=True)
    return (x * lax.rsqrt(ms + RMS_EPS)) * g


def _sigmoid(x):
    return 1.0 / (1.0 + jnp.exp(-x))


def _gelu_tanh(x):
    c = math.sqrt(2.0 / math.pi)
    return x * (0.5 * (1.0 + jnp.tanh(c * (x + 0.044715 * (x * x * x)))))


def _softplus(z):
    return jnp.maximum(z, 0.0) + jnp.log1p(jnp.exp(-jnp.abs(z)))


def _neg_expm1_2x(x):
    t = jnp.tanh(x)
    return (-2.0 * t) / (1.0 - t)


def _ffn_kernel(x_ref, g_ref, wg_ref, wu_ref, wd_ref, g2_ref, o_ref, *rest, nj, emit_x):
    if emit_x:
        h2_ref, hn_ref = rest
    else:
        (hn_ref,) = rest
    j = pl.program_id(1)

    @pl.when(j == 0)
    def _():
        hn_ref[...] = _rms(x_ref[...], g_ref[...]).astype(BF16)
        o_ref[...] = jnp.zeros_like(o_ref)

    hn = hn_ref[...]
    a = jnp.dot(hn, wg_ref[...], preferred_element_type=F32)
    b = jnp.dot(hn, wu_ref[...], preferred_element_type=F32)
    act = ((a * _sigmoid(a)) * b).astype(BF16)
    o_ref[...] += jnp.dot(act, wd_ref[...], preferred_element_type=F32)

    @pl.when(j == nj - 1)
    def _():
        xn = x_ref[...] + 0.5 * o_ref[...]
        if emit_x:
            o_ref[...] = xn
            h2_ref[...] = _rms(xn, g2_ref[...]).astype(BF16)
        else:
            o_ref[...] = _rms(xn, g2_ref[...])


def _ffn(x, g, wg, wu, wd, g2, *, tm, tf, emit_x):
    M, D = x.shape
    Fd = wg.shape[1]
    nj = Fd // tf
    out_shape = [jax.ShapeDtypeStruct((M, D), F32)]
    out_specs = [pl.BlockSpec((tm, D), lambda i, j: (i, 0))]
    if emit_x:
        out_shape.append(jax.ShapeDtypeStruct((M, D), BF16))
        out_specs.append(pl.BlockSpec((tm, D), lambda i, j: (i, 0)))
    res = pl.pallas_call(
        functools.partial(_ffn_kernel, nj=nj, emit_x=emit_x),
        grid=(M // tm, nj),
        in_specs=[
            pl.BlockSpec((tm, D), lambda i, j: (i, 0)),
            pl.BlockSpec((1, D), lambda i, j: (0, 0)),
            pl.BlockSpec((D, tf), lambda i, j: (0, j)),
            pl.BlockSpec((D, tf), lambda i, j: (0, j)),
            pl.BlockSpec((tf, D), lambda i, j: (j, 0)),
            pl.BlockSpec((1, D), lambda i, j: (0, 0)),
        ],
        out_specs=out_specs,
        out_shape=out_shape,
        scratch_shapes=[pltpu.VMEM((tm, D), BF16)],
        compiler_params=_cparams(("parallel", "arbitrary")),
        name="ffn_x" if emit_x else "ffn_final",
    )(x, g, wg, wu, wd, g2)
    return res if emit_x else res[0]


def _inproj_kernel(h_ref, w_ref, q_ref, k_ref, v_ref, k8_ref, v8_ref, xl_ref, gt_ref):
    j = pl.program_id(1)
    tm, width = k_ref.shape
    H = width // HEAD_DIM
    r = jnp.dot(h_ref[...], w_ref[...], preferred_element_type=F32)

    def head_rows(dst_ref):
        for h in range(H):
            dst_ref[pl.ds(h, tm, stride=H), :] = r[:, h * HEAD_DIM:(h + 1) * HEAD_DIM]

    @pl.when(j == 0)
    def _():
        q_ref[...] = r * ATT_SCALE

    @pl.when(j == 1)
    def _():
        k_ref[...] = r
        head_rows(k8_ref)

    @pl.when(j == 2)
    def _():
        v_ref[...] = r
        head_rows(v8_ref)

    @pl.when(j == 3)
    def _():
        xl_ref[...] = r

    @pl.when(j == 4)
    def _():
        gt_ref[...] = r


def _inproj(h, w_in, *, tm, width):
    M, D = h.shape
    assert w_in.shape[1] == 5 * width
    H = width // HEAD_DIM
    ospec = pl.BlockSpec((tm, width), lambda i, j: (i, 0))
    hspec = pl.BlockSpec((tm * H, HEAD_DIM), lambda i, j: (i, 0))
    tok = jax.ShapeDtypeStruct((M, width), F32)
    rows = jax.ShapeDtypeStruct((M * H, HEAD_DIM), F32)
    return pl.pallas_call(
        _inproj_kernel,
        grid=(M // tm, 5),
        in_specs=[
            pl.BlockSpec((tm, D), lambda i, j: (i, 0)),
            pl.BlockSpec((D, width), lambda i, j: (0, j)),
        ],
        out_specs=[ospec, ospec, ospec, hspec, hspec, ospec, ospec],
        out_shape=[tok, tok, tok, rows, rows, tok, tok],
        compiler_params=_cparams(("parallel", "arbitrary")),
        name="inproj",
    )(h, w_in)


def _rel_bucket(dist):
    max_exact = N_BUCKETS // 2
    df = jnp.maximum(dist, 1).astype(F32)
    large = max_exact + (jnp.log(df / max_exact) / math.log(MAX_DISTANCE / max_exact)
                         * (N_BUCKETS - max_exact)).astype(jnp.int32)
    return jnp.where(dist < max_exact, dist, jnp.minimum(large, N_BUCKETS - 1))


def _band_bias_t(rel_bias, window, dil):
    n = window // dil
    assert n == LANE
    bias_j = rel_bias[_rel_bucket(dil * jnp.arange(n + 1, dtype=jnp.int32))].T.astype(F32)
    H = bias_j.shape[0]
    neg = jnp.full((H, n - 1), NEG_INF, F32)
    lg = 3 * n - 1
    wext = jnp.concatenate([neg, bias_j, neg, jnp.zeros((H, 1), F32)], axis=1)
    skew = jnp.tile(wext, (1, 2 * n))[:, :2 * n * lg].reshape(H, 2 * n, lg)
    return skew[:, :, 2 * n - 1:2 * n - 1 + n]


def _attn_kernel(q_ref, k_ref, v_ref, bias_ref, *rest, S, dils):
    npat = len(dils)
    o_refs = rest[:npat]
    lse_ref, acc_s = rest[npat:]
    nt = (((1,), (1,)), ((), ()))
    tn = (((0,), (0,)), ((), ()))

    def rows(first, count, d):
        return pl.ds(first, count) if d == 1 else pl.ds(first, count, stride=d)

    unit = 0
    for p, d in enumerate(dils):
        nblk = S // (d * LANE)
        for r in range(d):
            for c in range(nblk):
                qrows = rows(r + d * c * LANE, LANE, d)
                q = q_ref[0, qrows, :].astype(BF16)
                if c == 0:
                    krows = qrows
                    bias = bias_ref[p, 0, LANE:2 * LANE, :]
                else:
                    krows = rows(r + d * (c - 1) * LANE, 2 * LANE, d)
                    bias = bias_ref[p, 0]
                k2 = k_ref[0, krows, :].astype(BF16)
                v2 = v_ref[0, krows, :].astype(BF16)
                st = lax.dot_general(k2, q, nt, preferred_element_type=F32) + bias
                m = jnp.max(st, axis=0, keepdims=True)
                e = jnp.exp(st - m)
                l = jnp.sum(e, axis=0, keepdims=True)
                pn = (e * (1.0 / l)).astype(BF16)
                acc_s[p, qrows, :] = lax.dot_general(pn, v2, tn, preferred_element_type=F32)
                lse_ref[0, 0, unit:unit + 1, :] = m + jnp.log(l)
                unit += 1
    for p in range(npat):
        o_refs[p][0] = acc_s[p].astype(o_refs[p].dtype)


def _attn_prompt(q, k, v, bias_t, *, B, S, H, dils):
    npat = len(dils)
    width = H * HEAD_DIM
    units = sum(S // LANE for _ in dils)
    col = pl.BlockSpec((1, S, HEAD_DIM), lambda b, h: (b, 0, h))
    q3, k3, v3 = (a.reshape(B, S, width) for a in (q, k, v))
    res = pl.pallas_call(
        functools.partial(_attn_kernel, S=S, dils=dils),
        grid=(B, H),
        in_specs=[col, col, col,
                  pl.BlockSpec((npat, 1, 2 * LANE, LANE), lambda b, h: (0, h, 0, 0))],
        out_specs=[col] * npat + [pl.BlockSpec((1, 1, units, LANE), lambda b, h: (b, h, 0, 0))],
        out_shape=[jax.ShapeDtypeStruct((B, S, width), BF16)] * npat
        + [jax.ShapeDtypeStruct((B, H, units, LANE), F32)],
        scratch_shapes=[pltpu.VMEM((npat, S, HEAD_DIM), F32)],
        compiler_params=_cparams(("parallel", "parallel")),
        name="attn_prompt",
    )(q3, k3, v3, bias_t)
    outs = [o.reshape(B * S, width) for o in res[:npat]]
    lses = []
    u0 = 0
    for d in dils:
        nblk = S // (d * LANE)
        sub = res[npat][:, :, u0:u0 + d * nblk, :].reshape(B, H, d, nblk, LANE)
        lses.append(sub.transpose(0, 3, 4, 2, 1).reshape(B * S, H))
        u0 += d * nblk
    return outs, jnp.stack(lses)


def _lru_gate_block(un, n, wax_ref, ba_ref, bx_ref, sp):
    sl = slice(n * LRU_BLOCK_W, (n + 1) * LRU_BLOCK_W)
    gx = jnp.dot(un.astype(BF16), wax_ref[n], preferred_element_type=F32)
    r = _sigmoid(gx[:, :LRU_BLOCK_W] + ba_ref[:, sl])
    i = _sigmoid(gx[:, LRU_BLOCK_W:] + bx_ref[:, sl])
    log_a = (-LRU_C * r) * sp[:, sl]
    return jnp.exp(log_a), jnp.sqrt(_neg_expm1_2x(log_a)) * (i * un)


def _lru_kernel(xl_ref, gt_ref, cw_ref, cb_ref, wax_ref, ba_ref, bx_ref, lam_ref, g_ref,
                yn_ref, hl_ref, xc_s, a_s, b_s, h_s, *, B, tT):
    t = pl.program_id(0)
    R = xl_ref.shape[-1]
    NB = R // LRU_BLOCK_W
    pad = 8

    @pl.when(t == 0)
    def _():
        xc_s[:, 0:pad, :] = jnp.zeros((B, pad, R), F32)
        h_s[...] = jnp.zeros_like(h_s)

    xc_s[:, pad:, :] = xl_ref[...]
    u = cb_ref[...][None]
    for j in range(CONV_WIDTH):
        off = pad - (CONV_WIDTH - 1) + j
        u = u + cw_ref[j:j + 1, :][None] * xc_s[:, off:off + tT, :]
    xc_s[:, pad - (CONV_WIDTH - 1):pad, :] = xc_s[:, pad + tT - (CONV_WIDTH - 1):pad + tT, :]

    sp = _softplus(-lam_ref[...])
    u2 = u.reshape(B * tT, R)
    for n in range(NB):
        a, bx = _lru_gate_block(u2[:, n * LRU_BLOCK_W:(n + 1) * LRU_BLOCK_W], n, wax_ref, ba_ref, bx_ref, sp)
        a_s[n] = a
        b_s[n] = bx

    def step(tt, hs):
        rows = pl.ds(tt, B, stride=tT)
        new = []
        for n in range(NB):
            hn = a_s[n, rows, :] * hs[n] + b_s[n, rows, :]
            b_s[n, rows, :] = hn
            new.append(hn)
        return tuple(new)

    hs = lax.fori_loop(0, tT, step, tuple(h_s[n] for n in range(NB)))
    for n in range(NB):
        h_s[n] = hs[n]
        hl_ref[:, n * LRU_BLOCK_W:(n + 1) * LRU_BLOCK_W] = hs[n]

    hseq = jnp.concatenate([b_s[n] for n in range(NB)], axis=1)
    y = hseq * _gelu_tanh(gt_ref[...].reshape(B * tT, R))
    yn_ref[...] = _rms(y, g_ref[...]).astype(BF16).reshape(B, tT, R)


def _lru_prompt(xl, gate, cw, cb, wax, ba, bx, lam, g, *, B, T, tT):
    R = xl.shape[-1]
    xl3 = xl.reshape(B, T, R)
    gt3 = gate.reshape(B, T, R)
    row = lambda shp: pl.BlockSpec(shp, lambda t: (0,) * len(shp))
    blk = pl.BlockSpec((B, tT, R), lambda t: (0, t, 0))
    yn, hl = pl.pallas_call(
        functools.partial(_lru_kernel, B=B, tT=tT),
        grid=(T // tT,),
        in_specs=[blk, blk, row(cw.shape), row(cb.shape), row(wax.shape), row(ba.shape),
                  row(bx.shape), row(lam.shape), row(g.shape)],
        out_specs=[blk, row((B, R))],
        out_shape=[jax.ShapeDtypeStruct((B, T, R), BF16), jax.ShapeDtypeStruct((B, R), F32)],
        scratch_shapes=[pltpu.VMEM((B, tT + 8, R), F32),
                        pltpu.VMEM((R // LRU_BLOCK_W, B * tT, LRU_BLOCK_W), F32),
                        pltpu.VMEM((R // LRU_BLOCK_W, B * tT, LRU_BLOCK_W), F32),
                        pltpu.VMEM((R // LRU_BLOCK_W, B, LRU_BLOCK_W), F32)],
        compiler_params=_cparams(("arbitrary",)),
        name="lru_prompt",
    )(xl3, gt3, cw, cb, wax, ba, bx, lam, g)
    return yn.reshape(B * T, R), hl


def _lru_sample_kernel(xc_ref, gt_ref, h0_ref, cw_ref, cb_ref, wax_ref, ba_ref, bx_ref, lam_ref,
                       g_ref, yn_ref, hl_ref, *, B, T):
    NB = xc_ref.shape[0]
    L = CONV_WIDTH - 1 + T
    sp = _softplus(-lam_ref[...])
    hs = [h0_ref[:, n * LRU_BLOCK_W:(n + 1) * LRU_BLOCK_W] for n in range(NB)]
    for t in range(T):
        ys = []
        for n in range(NB):
            sl = slice(n * LRU_BLOCK_W, (n + 1) * LRU_BLOCK_W)
            un = cb_ref[:, sl]
            for j in range(CONV_WIDTH):
                un = un + cw_ref[j:j + 1, sl] * xc_ref[n, pl.ds(t + j, B, stride=L), :]
            a, bx = _lru_gate_block(un, n, wax_ref, ba_ref, bx_ref, sp)
            hs[n] = a * hs[n] + bx
            ys.append(hs[n] * _gelu_tanh(gt_ref[n, pl.ds(t, B, stride=T), :]))
        yn = _rms(jnp.concatenate(ys, axis=1), g_ref[...])
        for n in range(NB):
            yn_ref[n, pl.ds(t, B, stride=T), :] = yn[:, n * LRU_BLOCK_W:(n + 1) * LRU_BLOCK_W]
    for n in range(NB):
        hl_ref[:, n * LRU_BLOCK_W:(n + 1) * LRU_BLOCK_W] = hs[n]


def _lru_sample(xc, gate, h0, cw, cb, wax, ba, bx, lam, g, *, B, T):
    R = gate.shape[-1]
    NB = R // LRU_BLOCK_W
    split = lambda a: a.reshape(-1, NB, LRU_BLOCK_W).transpose(1, 0, 2)
    yn, hl = pl.pallas_call(
        functools.partial(_lru_sample_kernel, B=B, T=T),
        out_shape=[jax.ShapeDtypeStruct((NB, B * T, LRU_BLOCK_W), F32),
                   jax.ShapeDtypeStruct((B, R), F32)],
        compiler_params=pltpu.CompilerParams(vmem_limit_bytes=VMEM_LIMIT),
        name="lru_sample",
    )(split(xc), split(gate), h0, cw, cb, wax, ba, bx, lam, g)
    return yn.transpose(1, 0, 2).reshape(B * T, R), hl


def _attn_sample_kernel(q_ref, kn_ref, vn_ref, ck_ref, cv_ref, bias_ref, mult_ref, o_ref,
                        kb_s, vb_s):
    nbuf = ck_ref.shape[1]
    nnew = kn_ref.shape[1]
    npad = kb_s.shape[0] - nbuf - nnew
    kb_s[0:nbuf, :] = ck_ref[0].astype(BF16)
    vb_s[0:nbuf, :] = cv_ref[0].astype(BF16)
    kb_s[nbuf:nbuf + nnew, :] = kn_ref[0]
    vb_s[nbuf:nbuf + nnew, :] = vn_ref[0]
    kb_s[nbuf + nnew:, :] = jnp.zeros((npad, HEAD_DIM), BF16)
    vb_s[nbuf + nnew:, :] = jnp.zeros((npad, HEAD_DIM), BF16)

    s = lax.dot_general(q_ref[0], kb_s[...], (((1,), (1,)), ((), ())), preferred_element_type=F32)
    s = s + bias_ref[...]
    m = jnp.max(s, axis=-1, keepdims=True)
    p = jnp.exp(s - m) * mult_ref[...].astype(F32)
    l = jnp.sum(p, axis=-1, keepdims=True)
    pn = (p * (1.0 / l)).astype(BF16)
    o_ref[0] = jnp.dot(pn, vb_s[...], preferred_element_type=F32)


def _sample_tables(rel_bias, *, H, T, buf, nslots):
    slot = np.arange(nslots)[None, :]
    tq = np.arange(T)[:, None]
    dist = np.where(slot < buf, buf + tq - slot, tq - (slot - buf))
    real = (slot < buf + T) & (dist >= 0)
    mult = np.zeros((T, nslots), np.float32)
    for window, dil in DILATED_PATTERNS:
        mult += (real & (dist % dil == 0) & (dist // dil <= window // dil)).astype(np.float32)
    bucket = _rel_bucket(jnp.asarray(np.maximum(dist, 0), dtype=jnp.int32))
    bias = rel_bias[bucket].astype(F32).transpose(2, 0, 1)
    same_head = np.eye(H, dtype=bool)[:, None, None, :]
    keep = jnp.asarray(same_head & (mult > 0)[None, :, :, None])
    bias = jnp.where(keep, bias[..., None], NEG_INF).reshape(H * T, nslots * H)
    mult = np.broadcast_to(mult[None, :, :, None], (H, T, nslots, H)).reshape(H * T, nslots * H)
    return bias, jnp.asarray(mult, dtype=BF16)


def _attn_sample(q, k8, v8, cache_k, cache_v, rel_bias, *, B, T, H):
    buf = cache_k.shape[1]
    nslots = buf + 16
    qr = q.reshape(B, T, H, HEAD_DIM).transpose(0, 2, 1, 3).reshape(B, H * T, HEAD_DIM).astype(BF16)
    new = lambda a: a.reshape(B, T * H, HEAD_DIM).astype(BF16)
    bias, mult = _sample_tables(rel_bias, H=H, T=T, buf=buf, nslots=nslots)
    table = pl.BlockSpec((H * T, nslots * H), lambda b: (0, 0), pipeline_mode=pl.Buffered(1))
    per_b = lambda r: pl.BlockSpec((1, r, HEAD_DIM), lambda b: (b, 0, 0))
    o = pl.pallas_call(
        _attn_sample_kernel,
        grid=(B,),
        in_specs=[per_b(H * T), per_b(T * H), per_b(T * H), per_b(buf * H), per_b(buf * H), table, table],
        out_specs=per_b(H * T),
        out_shape=jax.ShapeDtypeStruct((B, H * T, HEAD_DIM), F32),
        scratch_shapes=[pltpu.VMEM((nslots * H, HEAD_DIM), BF16)] * 2,
        compiler_params=_cparams(("parallel",)),
        name="attn_sample",
    )(qr, new(k8), new(v8), cache_k.reshape(B, buf * H, HEAD_DIM), cache_v.reshape(B, buf * H, HEAD_DIM),
      bias, mult)
    return o.reshape(B, H, T, HEAD_DIM).transpose(0, 2, 1, 3).reshape(B * T, H * HEAD_DIM)


def _outproj_kernel(*refs, npat, H):
    x_ref = refs[0]
    o_refs = refs[1:1 + npat]
    k = 1 + npat
    if npat > 1:
        lse_ref = refs[k]
        k += 1
    yn_ref, g_ref, w_ref, out_ref = refs[k:k + 4]
    aw = w_ref.shape[0] // 2
    if npat > 1:
        ls = [lse_ref[p] for p in range(npat)]
        m = functools.reduce(jnp.maximum, ls)
        es = [jnp.exp(l - m) for l in ls]
        inv = 1.0 / functools.reduce(lambda a, b: a + b, es)
        al = [e * inv for e in es]
        parts = []
        for h in range(H):
            sl = slice(h * HEAD_DIM, (h + 1) * HEAD_DIM)
            acc = al[0][:, h:h + 1] * o_refs[0][:, sl].astype(F32)
            for p in range(1, npat):
                acc = acc + al[p][:, h:h + 1] * o_refs[p][:, sl].astype(F32)
            parts.append(acc)
        oatt = jnp.concatenate(parts, axis=1)
    else:
        oatt = o_refs[0][...].astype(F32)
    an = _rms(oatt, g_ref[...]).astype(BF16)
    acc = jnp.dot(an, w_ref[0:aw, :], preferred_element_type=F32)
    acc = acc + jnp.dot(yn_ref[...].astype(BF16), w_ref[aw:, :], preferred_element_type=F32)
    out_ref[...] = x_ref[...] + acc


def _outproj(x, outs, lse, yn, g_att, w_out, *, tm, H):
    M, D = x.shape
    npat = len(outs)
    aw = outs[0].shape[1]
    tok = lambda w: pl.BlockSpec((tm, w), lambda i: (i, 0))
    in_specs = [tok(D)] + [tok(aw)] * npat
    args = [x] + list(outs)
    if npat > 1:
        in_specs.append(pl.BlockSpec((npat, tm, H), lambda i: (0, i, 0)))
        args.append(lse)
    in_specs += [tok(yn.shape[1]), pl.BlockSpec((1, aw), lambda i: (0, 0)),
                 pl.BlockSpec(w_out.shape, lambda i: (0, 0))]
    args += [yn, g_att, w_out]
    return pl.pallas_call(
        functools.partial(_outproj_kernel, npat=npat, H=H),
        grid=(M // tm,),
        in_specs=in_specs,
        out_specs=tok(D),
        out_shape=jax.ShapeDtypeStruct((M, D), F32),
        compiler_params=_cparams(("parallel",)),
        name=f"outproj{npat}",
    )(*args)


def kernel(x_prompt, x_sample, cache_k, cache_v, state_conv, state_h, g_ffn1, w1_gate, w1_up, w1_down, g_mix, w_in, conv_w, conv_b, w_a, b_a, w_x, b_x, lam, rel_bias, g_att_out, g_lru_out, w_out, g_ffn2, w2_gate, w2_up, w2_down, g_final):
    B, S, D = x_prompt.shape
    Bs, Ts, _ = x_sample.shape
    depth = g_ffn1.shape[0]
    assert depth == 1
    R = conv_w.shape[-1]
    H = rel_bias.shape[1]
    width = H * HEAD_DIM
    buf = cache_k.shape[2]
    l = 0

    row = lambda a: a.reshape(1, -1)
    bf = lambda a: a.astype(BF16)
    wg1, wu1, wd1 = bf(w1_gate[l]), bf(w1_up[l]), bf(w1_down[l])
    wg2, wu2, wd2 = bf(w2_gate[l]), bf(w2_up[l]), bf(w2_down[l])
    win, wout = bf(w_in[l]), bf(w_out[l])
    wax = bf(jnp.concatenate([w_a[l], w_x[l]], axis=-1))
    lru_w = (conv_w[l], row(conv_b[l]), wax, row(b_a[l]), row(b_x[l]), row(lam[l]), row(g_lru_out[l]))
    tf = 512

    def trunk_in(x2d, tm):
        x1, hmix = _ffn(x2d, row(g_ffn1[l]), wg1, wu1, wd1, row(g_mix[l]), tm=tm, tf=tf, emit_x=True)
        return (x1,) + tuple(_inproj(hmix, win, tm=tm, width=width))

    def trunk_out(x1, outs, lse, yn, tm):
        x2 = _outproj(x1, outs, lse, yn, row(g_att_out[l]), wout, tm=tm, H=H)
        return _ffn(x2, row(g_ffn2[l]), wg2, wu2, wd2, row(g_final), tm=tm, tf=tf, emit_x=False)

    Mp = B * S
    x1, q, k, v, k8, v8, xl, gate = trunk_in(x_prompt.reshape(Mp, D), 512)
    dils = tuple(d for _, d in DILATED_PATTERNS)
    bias_t = jnp.stack([_band_bias_t(rel_bias, w, d) for w, d in DILATED_PATTERNS])
    outs, lses = _attn_prompt(q, k, v, bias_t, B=B, S=S, H=H, dils=dils)
    yn, h_p = _lru_prompt(xl, gate, *lru_w, B=B, T=S, tT=64)
    y_prompt = trunk_out(x1, outs, lses, yn, 512).reshape(B, S, D)
    keep = min(MAX_WINDOW, S)
    k_prompt = k8.reshape(B, S, H, HEAD_DIM)[:, S - keep:][None]
    v_prompt = v8.reshape(B, S, H, HEAD_DIM)[:, S - keep:][None]
    conv_prompt = xl.reshape(B, S, R)[:, S - (CONV_WIDTH - 1):][None]

    Ms = Bs * Ts
    x1s, qs, _, _, k8s, v8s, xls, gates = trunk_in(x_sample.reshape(Ms, D), Ms)
    o_s = _attn_sample(qs, k8s, v8s, cache_k[l], cache_v[l], rel_bias, B=Bs, T=Ts, H=H)
    xc = jnp.concatenate([state_conv[l], xls.reshape(Bs, Ts, R)], axis=1)
    yns, h_s = _lru_sample(xc, gates, state_h[l], *lru_w, B=Bs, T=Ts)
    y_sample = trunk_out(x1s, [o_s], None, yns, Ms).reshape(Bs, Ts, D)
    k_sample = k8s.reshape(Bs, Ts, H, HEAD_DIM)[None]
    v_sample = v8s.reshape(Bs, Ts, H, HEAD_DIM)[None]
    conv_sample = xc[:, -(CONV_WIDTH - 1):][None]

    return (y_prompt, y_sample, k_prompt, v_prompt, conv_prompt, h_p[None],
            k_sample, v_sample, conv_sample, h_s[None])
```

```python
import functools
import math

import jax
import jax.numpy as jnp
import numpy as np
from jax import lax
from jax.experimental import pallas as pl
from jax.experimental.pallas import tpu as pltpu

F32 = jnp.float32
BF16 = jnp.bfloat16

HEAD_DIM = 128
LRU_BLOCK_W = 128
CONV_WIDTH = 4
LRU_C = 8.0
DILATED_PATTERNS = ((128, 1), (512, 4), (2048, 16))
MAX_WINDOW = 2048
N_BUCKETS = 32
MAX_DISTANCE = MAX_WINDOW
RMS_EPS = 1e-6
NEG_INF = -1e30
ATT_SCALE = 1.0 / math.sqrt(HEAD_DIM)

LANE = 128
MIB = 1024 * 1024
VMEM_LIMIT = 56 * MIB


def _cparams(semantics):
    return pltpu.CompilerParams(dimension_semantics=semantics, vmem_limit_bytes=VMEM_LIMIT)


def _rms(x, g):
    ms = jnp.mean(x * x, axis=-1, keepdims=True)
    return (x * lax.rsqrt(ms + RMS_EPS)) * g


def _sigmoid(x):
    return 1.0 / (1.0 + jnp.exp(-x))


def _gelu_tanh(x):
    c = math.sqrt(2.0 / math.pi)
    return x * (0.5 * (1.0 + jnp.tanh(c * (x + 0.044715 * (x * x * x)))))


def _softplus(z):
    return jnp.maximum(z, 0.0) + jnp.log1p(jnp.exp(-jnp.abs(z)))


def _neg_expm1_2x(x):
    t = jnp.tanh(x)
    return (-2.0 * t) / (1.0 - t)


def _swiglu_accumulate(hn_ref, wg_ref, wu_ref, wd_ref, o_ref):
    hn = hn_ref[...]
    a = jnp.dot(hn, wg_ref[...], preferred_element_type=F32)
    b = jnp.dot(hn, wu_ref[...], preferred_element_type=F32)
    act = ((a * _sigmoid(a)) * b).astype(BF16)
    o_ref[...] += jnp.dot(act, wd_ref[...], preferred_element_type=F32)


def _ffn_in_kernel(x_ref, g_ref, wg_ref, wu_ref, wd_ref, g2_ref, o_ref, h2_ref, hn_ref, *, nj):
    j = pl.program_id(1)

    @pl.when(j == 0)
    def _():
        hn_ref[...] = _rms(x_ref[...], g_ref[...]).astype(BF16)
        o_ref[...] = jnp.zeros_like(o_ref)

    _swiglu_accumulate(hn_ref, wg_ref, wu_ref, wd_ref, o_ref)

    @pl.when(j == nj - 1)
    def _():
        xn = x_ref[...] + 0.5 * o_ref[...]
        o_ref[...] = xn
        h2_ref[...] = _rms(xn, g2_ref[...]).astype(BF16)


def _ffn_in(x, g, wg, wu, wd, g2, *, tm, tf):
    M, D = x.shape
    nj = wg.shape[1] // tf
    tok = pl.BlockSpec((tm, D), lambda i, j: (i, 0))
    vec = pl.BlockSpec((1, D), lambda i, j: (0, 0))
    return pl.pallas_call(
        functools.partial(_ffn_in_kernel, nj=nj),
        grid=(M // tm, nj),
        in_specs=[tok, vec,
                  pl.BlockSpec((D, tf), lambda i, j: (0, j)),
                  pl.BlockSpec((D, tf), lambda i, j: (0, j)),
                  pl.BlockSpec((tf, D), lambda i, j: (j, 0)),
                  vec],
        out_specs=[tok, tok],
        out_shape=[jax.ShapeDtypeStruct((M, D), F32), jax.ShapeDtypeStruct((M, D), BF16)],
        scratch_shapes=[pltpu.VMEM((tm, D), BF16)],
        compiler_params=_cparams(("parallel", "arbitrary")),
        name="ffn_in",
    )(x, g, wg, wu, wd, g2)


def _ffn_out_kernel(x_ref, oa_ref, yn_ref, ga_ref, wo_ref, g_ref, wg_ref, wu_ref, wd_ref, g2_ref,
                    o_ref, hn_ref, x2_ref, *, nj):
    j = pl.program_id(1)

    @pl.when(j == 0)
    def _():
        aw = oa_ref.shape[1]
        an = _rms(oa_ref[...], ga_ref[...]).astype(BF16)
        proj = jnp.dot(an, wo_ref[0:aw, :], preferred_element_type=F32)
        proj = proj + jnp.dot(yn_ref[...].astype(BF16), wo_ref[aw:, :], preferred_element_type=F32)
        x2 = x_ref[...] + proj
        x2_ref[...] = x2
        hn_ref[...] = _rms(x2, g_ref[...]).astype(BF16)
        o_ref[...] = jnp.zeros_like(o_ref)

    _swiglu_accumulate(hn_ref, wg_ref, wu_ref, wd_ref, o_ref)

    @pl.when(j == nj - 1)
    def _():
        o_ref[...] = _rms(x2_ref[...] + 0.5 * o_ref[...], g2_ref[...])


def _ffn_out(x, oatt, yn, g_att, w_out, g, wg, wu, wd, g2, *, tm, tf):
    M, D = x.shape
    nj = wg.shape[1] // tf
    tok = lambda w: pl.BlockSpec((tm, w), lambda i, j: (i, 0))
    vec = lambda w: pl.BlockSpec((1, w), lambda i, j: (0, 0))
    return pl.pallas_call(
        functools.partial(_ffn_out_kernel, nj=nj),
        grid=(M // tm, nj),
        in_specs=[tok(D), tok(oatt.shape[1]), tok(yn.shape[1]), vec(oatt.shape[1]),
                  pl.BlockSpec(w_out.shape, lambda i, j: (0, 0), pipeline_mode=pl.Buffered(1)),
                  vec(D),
                  pl.BlockSpec((D, tf), lambda i, j: (0, j)),
                  pl.BlockSpec((D, tf), lambda i, j: (0, j)),
                  pl.BlockSpec((tf, D), lambda i, j: (j, 0)),
                  vec(D)],
        out_specs=tok(D),
        out_shape=jax.ShapeDtypeStruct((M, D), F32),
        scratch_shapes=[pltpu.VMEM((tm, D), BF16), pltpu.VMEM((tm, D), F32)],
        compiler_params=_cparams(("parallel", "arbitrary")),
        name="ffn_out",
    )(x, oatt, yn, g_att, w_out, g, wg, wu, wd, g2)


def _inproj_kernel(h_ref, w_ref, q_ref, k_ref, v_ref, k8_ref, v8_ref, xl_ref, gt_ref):
    j = pl.program_id(1)
    tm, width = k_ref.shape
    H = width // HEAD_DIM

    def project():
        return jnp.dot(h_ref[...], w_ref[...], preferred_element_type=F32)

    def both_layouts(tok_ref, rows_ref):
        r = project()
        tok_ref[...] = r
        for h in range(H):
            rows_ref[pl.ds(h, tm, stride=H), :] = r[:, h * HEAD_DIM:(h + 1) * HEAD_DIM]

    @pl.when(j == 0)
    def _():
        q_ref[...] = project() * ATT_SCALE

    @pl.when(j == 1)
    def _():
        both_layouts(k_ref, k8_ref)

    @pl.when(j == 2)
    def _():
        both_layouts(v_ref, v8_ref)

    @pl.when(j == 3)
    def _():
        xl_ref[...] = project()

    @pl.when(j == 4)
    def _():
        gt_ref[...] = project()


def _inproj(h, w_in, *, tm, width):
    M, D = h.shape
    assert w_in.shape[1] == 5 * width
    H = width // HEAD_DIM
    ospec = pl.BlockSpec((tm, width), lambda i, j: (i, 0))
    hspec = pl.BlockSpec((tm * H, HEAD_DIM), lambda i, j: (i, 0))
    tok = jax.ShapeDtypeStruct((M, width), F32)
    rows = jax.ShapeDtypeStruct((M * H, HEAD_DIM), F32)
    return pl.pallas_call(
        _inproj_kernel,
        grid=(M // tm, 5),
        in_specs=[
            pl.BlockSpec((tm, D), lambda i, j: (i, 0)),
            pl.BlockSpec((D, width), lambda i, j: (0, j)),
        ],
        out_specs=[ospec, ospec, ospec, hspec, hspec, ospec, ospec],
        out_shape=[tok, tok, tok, rows, rows, tok, tok],
        compiler_params=_cparams(("parallel", "arbitrary")),
        name="inproj",
    )(h, w_in)


def _rel_bucket(dist):
    max_exact = N_BUCKETS // 2
    df = jnp.maximum(dist, 1).astype(F32)
    large = max_exact + (jnp.log(df / max_exact) / math.log(MAX_DISTANCE / max_exact)
                         * (N_BUCKETS - max_exact)).astype(jnp.int32)
    return jnp.where(dist < max_exact, dist, jnp.minimum(large, N_BUCKETS - 1))


def _band_bias_t(rel_bias, window, dil):
    n = window // dil
    assert n == LANE
    bias_j = rel_bias[_rel_bucket(dil * jnp.arange(n + 1, dtype=jnp.int32))].T.astype(F32)
    H = bias_j.shape[0]
    neg = jnp.full((H, n - 1), NEG_INF, F32)
    lg = 3 * n - 1
    wext = jnp.concatenate([neg, bias_j, neg, jnp.zeros((H, 1), F32)], axis=1)
    skew = jnp.tile(wext, (1, 2 * n))[:, :2 * n * lg].reshape(H, 2 * n, lg)
    return skew[:, :, 2 * n - 1:2 * n - 1 + n]


LOG2E = math.log2(math.e)
LN2 = math.log(2.0)
MIX_ROWS = 256


def _attn_kernel(q_ref, k_ref, v_ref, bias_ref, o_ref, qc_s, kc_s, vc_s, acc_s, lse_s, *, S, dils):
    npat = len(dils)
    nt = (((1,), (1,)), ((), ()))

    def rows(first, count, d):
        return pl.ds(first, count) if d == 1 else pl.ds(first, count, stride=d)

    for p, d in enumerate(dils):
        n = S // d
        vc_s[p, :, HEAD_DIM:] = jnp.ones((S, HEAD_DIM), BF16)
        for r in range(d):
            src = rows(r, n, d)
            dst = slice(r * n, (r + 1) * n)
            qc_s[p, dst, :] = (q_ref[0, src, :] * LOG2E).astype(BF16)
            kc_s[p, dst, :] = k_ref[0, src, :].astype(BF16)
            vc_s[p, dst, 0:HEAD_DIM] = v_ref[0, src, :].astype(BF16)

    for p, d in enumerate(dils):
        nblk = S // (d * LANE)
        for r in range(d):
            for c in range(nblk):
                u = r * nblk + c
                lo = u if c == 0 else u - 1
                q = qc_s[p, u * LANE:(u + 1) * LANE, :]
                k2 = kc_s[p, lo * LANE:(u + 1) * LANE, :]
                v2 = vc_s[p, lo * LANE:(u + 1) * LANE, :]
                bias = bias_ref[p, 0, :, LANE:] if c == 0 else bias_ref[p, 0]
                s2 = lax.dot_general(q, k2, nt, preferred_element_type=F32) + bias
                m2 = jnp.max(s2, axis=1, keepdims=True)
                e = jnp.exp2(s2 - m2).astype(BF16)
                oe = jnp.dot(e, v2, preferred_element_type=F32)
                den = oe[:, HEAD_DIM:]
                prow = rows(r + d * c * LANE, LANE, d)
                acc_s[p, prow, :] = oe[:, :HEAD_DIM] * (1.0 / den)
                lse_s[p, prow, :] = m2 * LN2 + jnp.log(den)

    for ch in range(S // MIX_ROWS):
        sl = slice(ch * MIX_ROWS, (ch + 1) * MIX_ROWS)
        ls = [lse_s[p, sl, :] for p in range(npat)]
        m = functools.reduce(jnp.maximum, ls)
        es = [jnp.exp(l - m) for l in ls]
        inv = 1.0 / functools.reduce(lambda a, b: a + b, es)
        out = (es[0] * inv) * acc_s[0, sl, :]
        for p in range(1, npat):
            out = out + (es[p] * inv) * acc_s[p, sl, :]
        o_ref[0, sl, :] = out


def _attn_prompt(q, k, v, bias_t, *, B, S, H, dils):
    npat = len(dils)
    width = H * HEAD_DIM
    col = pl.BlockSpec((1, S, HEAD_DIM), lambda b, h: (b, 0, h))
    q3, k3, v3 = (a.reshape(B, S, width) for a in (q, k, v))
    o = pl.pallas_call(
        functools.partial(_attn_kernel, S=S, dils=dils),
        grid=(B, H),
        in_specs=[col, col, col,
                  pl.BlockSpec((npat, 1, LANE, 2 * LANE), lambda b, h: (0, h, 0, 0))],
        out_specs=col,
        out_shape=jax.ShapeDtypeStruct((B, S, width), F32),
        scratch_shapes=[pltpu.VMEM((npat, S, HEAD_DIM), BF16), pltpu.VMEM((npat, S, HEAD_DIM), BF16),
                        pltpu.VMEM((npat, S, 2 * HEAD_DIM), BF16),
                        pltpu.VMEM((npat, S, HEAD_DIM), F32), pltpu.VMEM((npat, S, HEAD_DIM), F32)],
        compiler_params=_cparams(("parallel", "parallel")),
        name="attn_prompt",
    )(q3, k3, v3, bias_t)
    return o.reshape(B * S, width)


def _scan_pitch(rows):
    pitch = -(-rows // 8) * 8
    return pitch if (pitch // 8) % 2 else pitch + 8


def _lru_gate_block(un, n, wax_ref, ba_ref, bx_ref, sp):
    sl = slice(n * LRU_BLOCK_W, (n + 1) * LRU_BLOCK_W)
    gx = jnp.dot(un.astype(BF16), wax_ref[n], preferred_element_type=F32)
    r = _sigmoid(gx[:, :LRU_BLOCK_W] + ba_ref[:, sl])
    i = _sigmoid(gx[:, LRU_BLOCK_W:] + bx_ref[:, sl])
    log_a = (-LRU_C * r) * sp[:, sl]
    return jnp.exp(log_a), jnp.sqrt(_neg_expm1_2x(log_a)) * (i * un)


def _lru_kernel(xl_ref, gt_ref, cw_ref, cb_ref, wax_ref, ba_ref, bx_ref, lam_ref, g_ref,
                yn_ref, hl_ref, xc_s, a_s, b_s, h_s, *, B, tT):
    t = pl.program_id(0)
    R = xl_ref.shape[-1]
    NB = R // LRU_BLOCK_W
    pad = 8

    @pl.when(t == 0)
    def _():
        xc_s[:, 0:pad, :] = jnp.zeros((B, pad, R), F32)
        h_s[...] = jnp.zeros_like(h_s)

    xc_s[:, pad:, :] = xl_ref[...]
    u = cb_ref[...][None]
    for j in range(CONV_WIDTH):
        off = pad - (CONV_WIDTH - 1) + j
        u = u + cw_ref[j:j + 1, :][None] * xc_s[:, off:off + tT, :]
    xc_s[:, pad - (CONV_WIDTH - 1):pad, :] = xc_s[:, pad + tT - (CONV_WIDTH - 1):pad + tT, :]

    sp = _softplus(-lam_ref[...])
    u2 = u.reshape(B * tT, R)
    pitch = a_s.shape[1] // B
    for n in range(NB):
        a, bx = _lru_gate_block(u2[:, n * LRU_BLOCK_W:(n + 1) * LRU_BLOCK_W], n, wax_ref, ba_ref, bx_ref, sp)
        for b in range(B):
            a_s[n, b * pitch:b * pitch + tT, :] = a[b * tT:(b + 1) * tT]
            b_s[n, b * pitch:b * pitch + tT, :] = bx[b * tT:(b + 1) * tT]

    def step(tt, hs):
        rows = pl.ds(tt, B, stride=pitch)
        new = []
        for n in range(NB):
            hn = a_s[n, rows, :] * hs[n] + b_s[n, rows, :]
            b_s[n, rows, :] = hn
            new.append(hn)
        return tuple(new)

    hs = lax.fori_loop(0, tT, step, tuple(h_s[n] for n in range(NB)))
    for n in range(NB):
        h_s[n] = hs[n]
        hl_ref[:, n * LRU_BLOCK_W:(n + 1) * LRU_BLOCK_W] = hs[n]

    for b in range(B):
        hseq = jnp.concatenate([b_s[n, b * pitch:b * pitch + tT, :] for n in range(NB)], axis=1)
        y = hseq * _gelu_tanh(gt_ref[b])
        yn_ref[b] = _rms(y, g_ref[...]).astype(BF16)


def _lru_prompt(xl, gate, cw, cb, wax, ba, bx, lam, g, *, B, T, tT):
    R = xl.shape[-1]
    xl3 = xl.reshape(B, T, R)
    gt3 = gate.reshape(B, T, R)
    row = lambda shp: pl.BlockSpec(shp, lambda t: (0,) * len(shp))
    blk = pl.BlockSpec((B, tT, R), lambda t: (0, t, 0))
    yn, hl = pl.pallas_call(
        functools.partial(_lru_kernel, B=B, tT=tT),
        grid=(T // tT,),
        in_specs=[blk, blk, row(cw.shape), row(cb.shape), row(wax.shape), row(ba.shape),
                  row(bx.shape), row(lam.shape), row(g.shape)],
        out_specs=[blk, row((B, R))],
        out_shape=[jax.ShapeDtypeStruct((B, T, R), BF16), jax.ShapeDtypeStruct((B, R), F32)],
        scratch_shapes=[pltpu.VMEM((B, tT + 8, R), F32),
                        pltpu.VMEM((R // LRU_BLOCK_W, B * _scan_pitch(tT), LRU_BLOCK_W), F32),
                        pltpu.VMEM((R // LRU_BLOCK_W, B * _scan_pitch(tT), LRU_BLOCK_W), F32),
                        pltpu.VMEM((R // LRU_BLOCK_W, B, LRU_BLOCK_W), F32)],
        compiler_params=_cparams(("arbitrary",)),
        name="lru_prompt",
    )(xl3, gt3, cw, cb, wax, ba, bx, lam, g)
    return yn.reshape(B * T, R), hl


def _lru_sample_kernel(xc_ref, gt_ref, h0_ref, cw_ref, cb_ref, wax_ref, ba_ref, bx_ref, lam_ref,
                       g_ref, yn_ref, hl_ref, *, B, T):
    NB = xc_ref.shape[0]
    L = CONV_WIDTH - 1 + T
    sp = _softplus(-lam_ref[...])
    hs = [h0_ref[:, n * LRU_BLOCK_W:(n + 1) * LRU_BLOCK_W] for n in range(NB)]
    for t in range(T):
        ys = []
        for n in range(NB):
            sl = slice(n * LRU_BLOCK_W, (n + 1) * LRU_BLOCK_W)
            un = cb_ref[:, sl]
            for j in range(CONV_WIDTH):
                un = un + cw_ref[j:j + 1, sl] * xc_ref[n, pl.ds(t + j, B, stride=L), :]
            a, bx = _lru_gate_block(un, n, wax_ref, ba_ref, bx_ref, sp)
            hs[n] = a * hs[n] + bx
            ys.append(hs[n] * _gelu_tanh(gt_ref[n, pl.ds(t, B, stride=T), :]))
        yn = _rms(jnp.concatenate(ys, axis=1), g_ref[...])
        for n in range(NB):
            yn_ref[n, pl.ds(t, B, stride=T), :] = yn[:, n * LRU_BLOCK_W:(n + 1) * LRU_BLOCK_W]
    for n in range(NB):
        hl_ref[:, n * LRU_BLOCK_W:(n + 1) * LRU_BLOCK_W] = hs[n]


def _lru_sample(xc, gate, h0, cw, cb, wax, ba, bx, lam, g, *, B, T):
    R = gate.shape[-1]
    NB = R // LRU_BLOCK_W
    split = lambda a: a.reshape(-1, NB, LRU_BLOCK_W).transpose(1, 0, 2)
    yn, hl = pl.pallas_call(
        functools.partial(_lru_sample_kernel, B=B, T=T),
        out_shape=[jax.ShapeDtypeStruct((NB, B * T, LRU_BLOCK_W), F32),
                   jax.ShapeDtypeStruct((B, R), F32)],
        compiler_params=pltpu.CompilerParams(vmem_limit_bytes=VMEM_LIMIT),
        name="lru_sample",
    )(split(xc), split(gate), h0, cw, cb, wax, ba, bx, lam, g)
    return yn.transpose(1, 0, 2).reshape(B * T, R), hl


def _attn_sample_kernel(q_ref, kn_ref, vn_ref, ck_ref, cv_ref, bias_ref, mult_ref, o_ref,
                        kb_s, vb_s):
    nbuf = ck_ref.shape[1]
    nnew = kn_ref.shape[1]
    npad = kb_s.shape[0] - nbuf - nnew
    kb_s[0:nbuf, :] = ck_ref[0].astype(BF16)
    vb_s[0:nbuf, :] = cv_ref[0].astype(BF16)
    kb_s[nbuf:nbuf + nnew, :] = kn_ref[0]
    vb_s[nbuf:nbuf + nnew, :] = vn_ref[0]
    kb_s[nbuf + nnew:, :] = jnp.zeros((npad, HEAD_DIM), BF16)
    vb_s[nbuf + nnew:, :] = jnp.zeros((npad, HEAD_DIM), BF16)

    s = lax.dot_general(q_ref[0], kb_s[...], (((1,), (1,)), ((), ())), preferred_element_type=F32)
    s = s + bias_ref[...]
    m = jnp.max(s, axis=-1, keepdims=True)
    p = jnp.exp(s - m) * mult_ref[...].astype(F32)
    l = jnp.sum(p, axis=-1, keepdims=True)
    pn = (p * (1.0 / l)).astype(BF16)
    o_ref[0] = jnp.dot(pn, vb_s[...], preferred_element_type=F32)


def _sample_tables(rel_bias, *, H, T, buf, nslots):
    slot = np.arange(nslots)[None, :]
    tq = np.arange(T)[:, None]
    dist = np.where(slot < buf, buf + tq - slot, tq - (slot - buf))
    real = (slot < buf + T) & (dist >= 0)
    mult = np.zeros((T, nslots), np.float32)
    for window, dil in DILATED_PATTERNS:
        mult += (real & (dist % dil == 0) & (dist // dil <= window // dil)).astype(np.float32)
    bucket = _rel_bucket(jnp.asarray(np.maximum(dist, 0), dtype=jnp.int32))
    bias = rel_bias[bucket].astype(F32).transpose(2, 0, 1)
    same_head = np.eye(H, dtype=bool)[:, None, None, :]
    keep = jnp.asarray(same_head & (mult > 0)[None, :, :, None])
    bias = jnp.where(keep, bias[..., None], NEG_INF).reshape(H * T, nslots * H)
    mult = np.broadcast_to(mult[None, :, :, None], (H, T, nslots, H)).reshape(H * T, nslots * H)
    return bias, jnp.asarray(mult, dtype=BF16)


def _attn_sample(q, k8, v8, cache_k, cache_v, rel_bias, *, B, T, H):
    buf = cache_k.shape[1]
    nslots = buf + 16
    qr = q.reshape(B, T, H, HEAD_DIM).transpose(0, 2, 1, 3).reshape(B, H * T, HEAD_DIM).astype(BF16)
    new = lambda a: a.reshape(B, T * H, HEAD_DIM).astype(BF16)
    bias, mult = _sample_tables(rel_bias, H=H, T=T, buf=buf, nslots=nslots)
    table = pl.BlockSpec((H * T, nslots * H), lambda b: (0, 0), pipeline_mode=pl.Buffered(1))
    per_b = lambda r: pl.BlockSpec((1, r, HEAD_DIM), lambda b: (b, 0, 0))
    o = pl.pallas_call(
        _attn_sample_kernel,
        grid=(B,),
        in_specs=[per_b(H * T), per_b(T * H), per_b(T * H), per_b(buf * H), per_b(buf * H), table, table],
        out_specs=per_b(H * T),
        out_shape=jax.ShapeDtypeStruct((B, H * T, HEAD_DIM), F32),
        scratch_shapes=[pltpu.VMEM((nslots * H, HEAD_DIM), BF16)] * 2,
        compiler_params=_cparams(("parallel",)),
        name="attn_sample",
    )(qr, new(k8), new(v8), cache_k.reshape(B, buf * H, HEAD_DIM), cache_v.reshape(B, buf * H, HEAD_DIM),
      bias, mult)
    return o.reshape(B, H, T, HEAD_DIM).transpose(0, 2, 1, 3).reshape(B * T, H * HEAD_DIM)


def kernel(x_prompt, x_sample, cache_k, cache_v, state_conv, state_h, g_ffn1, w1_gate, w1_up, w1_down, g_mix, w_in, conv_w, conv_b, w_a, b_a, w_x, b_x, lam, rel_bias, g_att_out, g_lru_out, w_out, g_ffn2, w2_gate, w2_up, w2_down, g_final):
    B, S, D = x_prompt.shape
    Bs, Ts, _ = x_sample.shape
    depth = g_ffn1.shape[0]
    assert depth == 1
    R = conv_w.shape[-1]
    H = rel_bias.shape[1]
    width = H * HEAD_DIM
    buf = cache_k.shape[2]
    l = 0

    row = lambda a: a.reshape(1, -1)
    bf = lambda a: a.astype(BF16)
    wg1, wu1, wd1 = bf(w1_gate[l]), bf(w1_up[l]), bf(w1_down[l])
    wg2, wu2, wd2 = bf(w2_gate[l]), bf(w2_up[l]), bf(w2_down[l])
    win, wout = bf(w_in[l]), bf(w_out[l])
    wax = bf(jnp.concatenate([w_a[l], w_x[l]], axis=-1))
    lru_w = (conv_w[l], row(conv_b[l]), wax, row(b_a[l]), row(b_x[l]), row(lam[l]), row(g_lru_out[l]))
    tf = 512

    def trunk_in(x2d, tm):
        x1, hmix = _ffn_in(x2d, row(g_ffn1[l]), wg1, wu1, wd1, row(g_mix[l]), tm=tm, tf=tf)
        return (x1,) + tuple(_inproj(hmix, win, tm=tm, width=width))

    def trunk_out(x1, oatt, yn, tm):
        return _ffn_out(x1, oatt, yn, row(g_att_out[l]), wout, row(g_ffn2[l]), wg2, wu2, wd2,
                        row(g_final), tm=tm, tf=tf)

    Mp = B * S
    x1, q, k, v, k8, v8, xl, gate = trunk_in(x_prompt.reshape(Mp, D), 512)
    dils = tuple(d for _, d in DILATED_PATTERNS)
    bias_q = jnp.stack([_band_bias_t(rel_bias, w, d) for w, d in DILATED_PATTERNS])
    bias_q = jnp.swapaxes(bias_q, -1, -2) * LOG2E
    oatt = _attn_prompt(q, k, v, bias_q, B=B, S=S, H=H, dils=dils)
    yn, h_p = _lru_prompt(xl, gate, *lru_w, B=B, T=S, tT=64)
    y_prompt = trunk_out(x1, oatt, yn, 512).reshape(B, S, D)
    keep = min(MAX_WINDOW, S)
    k_prompt = k8.reshape(B, S, H, HEAD_DIM)[:, S - keep:][None]
    v_prompt = v8.reshape(B, S, H, HEAD_DIM)[:, S - keep:][None]
    conv_prompt = xl.reshape(B, S, R)[:, S - (CONV_WIDTH - 1):][None]

    Ms = Bs * Ts
    x1s, qs, _, _, k8s, v8s, xls, gates = trunk_in(x_sample.reshape(Ms, D), Ms)
    o_s = _attn_sample(qs, k8s, v8s, cache_k[l], cache_v[l], rel_bias, B=Bs, T=Ts, H=H)
    xc = jnp.concatenate([state_conv[l], xls.reshape(Bs, Ts, R)], axis=1)
    yns, h_s = _lru_sample(xc, gates, state_h[l], *lru_w, B=Bs, T=Ts)
    y_sample = trunk_out(x1s, o_s, yns, Ms).reshape(Bs, Ts, D)
    k_sample = k8s.reshape(Bs, Ts, H, HEAD_DIM)[None]
    v_sample = v8s.reshape(Bs, Ts, H, HEAD_DIM)[None]
    conv_sample = xc[:, -(CONV_WIDTH - 1):][None]

    return (y_prompt, y_sample, k_prompt, v_prompt, conv_prompt, h_p[None],
            k_sample, v_sample, conv_sample, h_s[None])
```

```python
import functools
import math

import jax
import jax.numpy as jnp
import numpy as np
from jax import lax
from jax.experimental import pallas as pl
from jax.experimental.pallas import tpu as pltpu

F32 = jnp.float32
BF16 = jnp.bfloat16

HEAD_DIM = 128
LRU_BLOCK_W = 128
CONV_WIDTH = 4
LRU_C = 8.0
DILATED_PATTERNS = ((128, 1), (512, 4), (2048, 16))
MAX_WINDOW = 2048
N_BUCKETS = 32
MAX_DISTANCE = MAX_WINDOW
RMS_EPS = 1e-6
NEG_INF = -1e30
ATT_SCALE = 1.0 / math.sqrt(HEAD_DIM)

LANE = 128
MIB = 1024 * 1024
VMEM_LIMIT = 56 * MIB


def _cparams(semantics):
    return pltpu.CompilerParams(dimension_semantics=semantics, vmem_limit_bytes=VMEM_LIMIT)


def _rms(x, g):
    ms = jnp.mean(x * x, axis=-1, keepdims=True)
    return (x * lax.rsqrt(ms + RMS_EPS)) * g


def _sigmoid(x):
    return 1.0 / (1.0 + jnp.exp(-x))


def _gelu_tanh(x):
    c = math.sqrt(2.0 / math.pi)
    return x * (0.5 * (1.0 + jnp.tanh(c * (x + 0.044715 * (x * x * x)))))


def _softplus(z):
    return jnp.maximum(z, 0.0) + jnp.log1p(jnp.exp(-jnp.abs(z)))


def _neg_expm1_2x(x):
    t = jnp.tanh(x)
    return (-2.0 * t) / (1.0 - t)


def _swiglu_accumulate(hn_ref, wg_ref, wu_ref, wd_ref, o_ref):
    hn = hn_ref[...]
    a = jnp.dot(hn, wg_ref[...], preferred_element_type=F32)
    b = jnp.dot(hn, wu_ref[...], preferred_element_type=F32)
    act = ((a * _sigmoid(a)) * b).astype(BF16)
    o_ref[...] += jnp.dot(act, wd_ref[...], preferred_element_type=F32)


def _ffn_in_kernel(x_ref, g_ref, wg_ref, wu_ref, wd_ref, g2_ref, o_ref, h2_ref, hn_ref, *, nj):
    j = pl.program_id(1)

    @pl.when(j == 0)
    def _():
        hn_ref[...] = _rms(x_ref[...], g_ref[...]).astype(BF16)
        o_ref[...] = jnp.zeros_like(o_ref)

    _swiglu_accumulate(hn_ref, wg_ref, wu_ref, wd_ref, o_ref)

    @pl.when(j == nj - 1)
    def _():
        xn = x_ref[...] + 0.5 * o_ref[...]
        o_ref[...] = xn
        h2_ref[...] = _rms(xn, g2_ref[...]).astype(BF16)


def _ffn_in(x, g, wg, wu, wd, g2, *, tm, tf):
    M, D = x.shape
    nj = wg.shape[1] // tf
    tok = pl.BlockSpec((tm, D), lambda i, j: (i, 0))
    vec = pl.BlockSpec((1, D), lambda i, j: (0, 0))
    return pl.pallas_call(
        functools.partial(_ffn_in_kernel, nj=nj),
        grid=(M // tm, nj),
        in_specs=[tok, vec,
                  pl.BlockSpec((D, tf), lambda i, j: (0, j)),
                  pl.BlockSpec((D, tf), lambda i, j: (0, j)),
                  pl.BlockSpec((tf, D), lambda i, j: (j, 0)),
                  vec],
        out_specs=[tok, tok],
        out_shape=[jax.ShapeDtypeStruct((M, D), F32), jax.ShapeDtypeStruct((M, D), BF16)],
        scratch_shapes=[pltpu.VMEM((tm, D), BF16)],
        compiler_params=_cparams(("parallel", "arbitrary")),
        name="ffn_in",
    )(x, g, wg, wu, wd, g2)


def _ffn_out_kernel(x_ref, oa_ref, yn_ref, ga_ref, wo_ref, g_ref, wg_ref, wu_ref, wd_ref, g2_ref,
                    o_ref, hn_ref, x2_ref, *, nj):
    j = pl.program_id(1)

    @pl.when(j == 0)
    def _():
        aw = oa_ref.shape[1]
        an = _rms(oa_ref[...], ga_ref[...]).astype(BF16)
        proj = jnp.dot(an, wo_ref[0:aw, :], preferred_element_type=F32)
        proj = proj + jnp.dot(yn_ref[...].astype(BF16), wo_ref[aw:, :], preferred_element_type=F32)
        x2 = x_ref[...] + proj
        x2_ref[...] = x2
        hn_ref[...] = _rms(x2, g_ref[...]).astype(BF16)
        o_ref[...] = jnp.zeros_like(o_ref)

    _swiglu_accumulate(hn_ref, wg_ref, wu_ref, wd_ref, o_ref)

    @pl.when(j == nj - 1)
    def _():
        o_ref[...] = _rms(x2_ref[...] + 0.5 * o_ref[...], g2_ref[...])


def _ffn_out(x, oatt, yn, g_att, w_out, g, wg, wu, wd, g2, *, tm, tf):
    M, D = x.shape
    nj = wg.shape[1] // tf
    tok = lambda w: pl.BlockSpec((tm, w), lambda i, j: (i, 0))
    vec = lambda w: pl.BlockSpec((1, w), lambda i, j: (0, 0))
    return pl.pallas_call(
        functools.partial(_ffn_out_kernel, nj=nj),
        grid=(M // tm, nj),
        in_specs=[tok(D), tok(oatt.shape[1]), tok(yn.shape[1]), vec(oatt.shape[1]),
                  pl.BlockSpec(w_out.shape, lambda i, j: (0, 0), pipeline_mode=pl.Buffered(1)),
                  vec(D),
                  pl.BlockSpec((D, tf), lambda i, j: (0, j)),
                  pl.BlockSpec((D, tf), lambda i, j: (0, j)),
                  pl.BlockSpec((tf, D), lambda i, j: (j, 0)),
                  vec(D)],
        out_specs=tok(D),
        out_shape=jax.ShapeDtypeStruct((M, D), F32),
        scratch_shapes=[pltpu.VMEM((tm, D), BF16), pltpu.VMEM((tm, D), F32)],
        compiler_params=_cparams(("parallel", "arbitrary")),
        name="ffn_out",
    )(x, oatt, yn, g_att, w_out, g, wg, wu, wd, g2)


def _inproj_kernel(h_ref, w_ref, q_ref, k_ref, v_ref, k8_ref, v8_ref, xl_ref, gt_ref):
    j = pl.program_id(1)
    tm, width = k_ref.shape
    H = width // HEAD_DIM

    def project():
        return jnp.dot(h_ref[...], w_ref[...], preferred_element_type=F32)

    def both_layouts(tok_ref, rows_ref):
        r = project()
        tok_ref[...] = r
        for h in range(H):
            rows_ref[pl.ds(h, tm, stride=H), :] = r[:, h * HEAD_DIM:(h + 1) * HEAD_DIM]

    @pl.when(j == 0)
    def _():
        q_ref[...] = project() * ATT_SCALE

    @pl.when(j == 1)
    def _():
        both_layouts(k_ref, k8_ref)

    @pl.when(j == 2)
    def _():
        both_layouts(v_ref, v8_ref)

    @pl.when(j == 3)
    def _():
        xl_ref[...] = project()

    @pl.when(j == 4)
    def _():
        gt_ref[...] = project()


def _inproj(h, w_in, *, tm, width):
    M, D = h.shape
    assert w_in.shape[1] == 5 * width
    H = width // HEAD_DIM
    ospec = pl.BlockSpec((tm, width), lambda i, j: (i, 0))
    hspec = pl.BlockSpec((tm * H, HEAD_DIM), lambda i, j: (i, 0))
    tok = jax.ShapeDtypeStruct((M, width), F32)
    rows = jax.ShapeDtypeStruct((M * H, HEAD_DIM), F32)
    return pl.pallas_call(
        _inproj_kernel,
        grid=(M // tm, 5),
        in_specs=[
            pl.BlockSpec((tm, D), lambda i, j: (i, 0)),
            pl.BlockSpec((D, width), lambda i, j: (0, j)),
        ],
        out_specs=[ospec, ospec, ospec, hspec, hspec, ospec, ospec],
        out_shape=[tok, tok, tok, rows, rows, tok, tok],
        compiler_params=_cparams(("parallel", "arbitrary")),
        name="inproj",
    )(h, w_in)


def _rel_bucket(dist):
    max_exact = N_BUCKETS // 2
    df = jnp.maximum(dist, 1).astype(F32)
    large = max_exact + (jnp.log(df / max_exact) / math.log(MAX_DISTANCE / max_exact)
                         * (N_BUCKETS - max_exact)).astype(jnp.int32)
    return jnp.where(dist < max_exact, dist, jnp.minimum(large, N_BUCKETS - 1))


def _band_bias_t(rel_bias, window, dil):
    n = window // dil
    assert n == LANE
    bias_j = rel_bias[_rel_bucket(dil * jnp.arange(n + 1, dtype=jnp.int32))].T.astype(F32)
    H = bias_j.shape[0]
    neg = jnp.full((H, n - 1), NEG_INF, F32)
    lg = 3 * n - 1
    wext = jnp.concatenate([neg, bias_j, neg, jnp.zeros((H, 1), F32)], axis=1)
    skew = jnp.tile(wext, (1, 2 * n))[:, :2 * n * lg].reshape(H, 2 * n, lg)
    return skew[:, :, 2 * n - 1:2 * n - 1 + n]


LOG2E = math.log2(math.e)
LN2 = math.log(2.0)
MIX_ROWS = 256


def _attn_kernel(q_ref, k_ref, v_ref, bias_ref, o_ref, qc_s, kc_s, vc_s, acc_s, lse_s, *, S, dils):
    npat = len(dils)
    nt = (((1,), (1,)), ((), ()))

    def rows(first, count, d):
        return pl.ds(first, count) if d == 1 else pl.ds(first, count, stride=d)

    for p, d in enumerate(dils):
        n = S // d
        vc_s[p, :, HEAD_DIM:] = jnp.ones((S, HEAD_DIM), BF16)
        for r in range(d):
            src = rows(r, n, d)
            dst = slice(r * n, (r + 1) * n)
            qc_s[p, dst, :] = (q_ref[0, src, :] * LOG2E).astype(BF16)
            kc_s[p, dst, :] = k_ref[0, src, :].astype(BF16)
            vc_s[p, dst, 0:HEAD_DIM] = v_ref[0, src, :].astype(BF16)

    for p, d in enumerate(dils):
        nblk = S // (d * LANE)
        for r in range(d):
            for c in range(nblk):
                u = r * nblk + c
                lo = u if c == 0 else u - 1
                q = qc_s[p, u * LANE:(u + 1) * LANE, :]
                k2 = kc_s[p, lo * LANE:(u + 1) * LANE, :]
                v2 = vc_s[p, lo * LANE:(u + 1) * LANE, :]
                bias = bias_ref[p, 0, :, LANE:] if c == 0 else bias_ref[p, 0]
                s2 = lax.dot_general(q, k2, nt, preferred_element_type=F32) + bias
                m2 = jnp.max(s2, axis=1, keepdims=True)
                e = jnp.exp2(s2 - m2).astype(BF16)
                oe = jnp.dot(e, v2, preferred_element_type=F32)
                den = oe[:, HEAD_DIM:]
                prow = rows(r + d * c * LANE, LANE, d)
                acc_s[p, prow, :] = oe[:, :HEAD_DIM] * (1.0 / den)
                lse_s[p, prow, :] = m2 * LN2 + jnp.log(den)

    for ch in range(S // MIX_ROWS):
        sl = slice(ch * MIX_ROWS, (ch + 1) * MIX_ROWS)
        ls = [lse_s[p, sl, :] for p in range(npat)]
        m = functools.reduce(jnp.maximum, ls)
        es = [jnp.exp(l - m) for l in ls]
        inv = 1.0 / functools.reduce(lambda a, b: a + b, es)
        out = (es[0] * inv) * acc_s[0, sl, :]
        for p in range(1, npat):
            out = out + (es[p] * inv) * acc_s[p, sl, :]
        o_ref[0, sl, :] = out


def _attn_prompt(q, k, v, bias_t, *, B, S, H, dils):
    npat = len(dils)
    width = H * HEAD_DIM
    col = pl.BlockSpec((1, S, HEAD_DIM), lambda b, h: (b, 0, h))
    q3, k3, v3 = (a.reshape(B, S, width) for a in (q, k, v))
    o = pl.pallas_call(
        functools.partial(_attn_kernel, S=S, dils=dils),
        grid=(B, H),
        in_specs=[col, col, col,
                  pl.BlockSpec((npat, 1, LANE, 2 * LANE), lambda b, h: (0, h, 0, 0))],
        out_specs=col,
        out_shape=jax.ShapeDtypeStruct((B, S, width), F32),
        scratch_shapes=[pltpu.VMEM((npat, S, HEAD_DIM), BF16), pltpu.VMEM((npat, S, HEAD_DIM), BF16),
                        pltpu.VMEM((npat, S, 2 * HEAD_DIM), BF16),
                        pltpu.VMEM((npat, S, HEAD_DIM), F32), pltpu.VMEM((npat, S, HEAD_DIM), F32)],
        compiler_params=_cparams(("parallel", "parallel")),
        name="attn_prompt",
    )(q3, k3, v3, bias_t)
    return o.reshape(B * S, width)


def _scan_pitch(rows):
    pitch = -(-rows // 8) * 8
    return pitch if (pitch // 8) % 2 else pitch + 8


def _lru_gate_dot(un, n, wax_ref):
    return jnp.dot(un.astype(BF16), wax_ref[n], preferred_element_type=F32)


def _lru_gate_act(gx, un, n, ba_ref, bx_ref, sp):
    sl = slice(n * LRU_BLOCK_W, (n + 1) * LRU_BLOCK_W)
    r = _sigmoid(gx[:, :LRU_BLOCK_W] + ba_ref[:, sl])
    i = _sigmoid(gx[:, LRU_BLOCK_W:] + bx_ref[:, sl])
    log_a = (-LRU_C * r) * sp[:, sl]
    z = _neg_expm1_2x(log_a)
    root = jnp.where(z > 0.0, z * lax.rsqrt(z), 0.0)
    return jnp.exp(log_a), root * (i * un)


def _mixer_in_kernel(h_ref, w_ref, cw_ref, cb_ref, wax_ref, ba_ref, bx_ref, lam_ref, g_ref,
                     q_ref, k_ref, v_ref, k8_ref, v8_ref, yn_ref, xt_ref, hl_ref,
                     xc_s, gt_s, u_s, a_s, b_s, h_s, *, B, tT):
    t = pl.program_id(0)
    j = pl.program_id(1)
    R = xc_s.shape[-1]
    NB = R // LRU_BLOCK_W
    H = k_ref.shape[-1] // HEAD_DIM
    pad = 8
    pitch = a_s.shape[1] // B

    CW = 2 * LRU_BLOCK_W
    nchunks = R // CW

    def project(nc):
        h2 = h_ref[...].reshape(B * tT, h_ref.shape[-1])
        return jnp.dot(h2, w_ref[:, nc * CW:(nc + 1) * CW], preferred_element_type=F32)

    def both_layouts(nc, r, tok_ref, rows_ref):
        tok_ref[:, :, nc * CW:(nc + 1) * CW] = r.reshape(B, tT, CW)
        for hh in range(CW // HEAD_DIM):
            h = nc * (CW // HEAD_DIM) + hh
            for b in range(B):
                rows_ref[b, pl.ds(h, tT, stride=H), :] = r[b * tT:(b + 1) * tT, hh * HEAD_DIM:(hh + 1) * HEAD_DIM]

    @pl.when(jnp.logical_and(t == 0, j == 0))
    def _():
        xc_s[:, 0:pad, :] = jnp.zeros((B, pad, R), F32)
        h_s[...] = jnp.zeros_like(h_s)

    @pl.when(j == 0)
    def _():
        for nc in range(nchunks):
            xc_s[:, pad:, nc * CW:(nc + 1) * CW] = project(nc).reshape(B, tT, CW)

    @pl.when(j == 1)
    def _():
        for nc in range(nchunks):
            cs = slice(nc * CW, (nc + 1) * CW)
            gt_s[:, cs] = project(nc)
            for b in range(B):
                u = cb_ref[:, cs]
                for c in range(CONV_WIDTH):
                    off = pad - (CONV_WIDTH - 1) + c
                    u = u + cw_ref[c:c + 1, cs] * xc_s[b, off:off + tT, cs]
                u_s[b * tT:(b + 1) * tT, cs] = u
        xt_ref[...] = xc_s[:, tT:tT + pad, :]
        xc_s[:, pad - (CONV_WIDTH - 1):pad, :] = xc_s[:, pad + tT - (CONV_WIDTH - 1):pad + tT, :]

    @pl.when(j == 2)
    def _():
        sp = _softplus(-lam_ref[...])
        for nc in range(nchunks):
            q_ref[:, :, nc * CW:(nc + 1) * CW] = (project(nc) * ATT_SCALE).reshape(B, tT, CW)
            for n in range(nc * (CW // LRU_BLOCK_W), (nc + 1) * (CW // LRU_BLOCK_W)):
                un = u_s[:, n * LRU_BLOCK_W:(n + 1) * LRU_BLOCK_W]
                gx = _lru_gate_dot(un, n, wax_ref)
                for b in range(B):
                    rows = slice(b * tT, (b + 1) * tT)
                    a, bx = _lru_gate_act(gx[rows], un[rows], n, ba_ref, bx_ref, sp)
                    a_s[n, b * pitch:b * pitch + tT, :] = a
                    b_s[n, b * pitch:b * pitch + tT, :] = bx

    @pl.when(j == 3)
    def _():
        for nc in range(nchunks):
            both_layouts(nc, project(nc), k_ref, k8_ref)

        def step(tt, hs):
            rows = pl.ds(tt, B, stride=pitch)
            new = []
            for n in range(NB):
                hn = a_s[n, rows, :] * hs[n] + b_s[n, rows, :]
                b_s[n, rows, :] = hn
                new.append(hn)
            return tuple(new)

        hs = lax.fori_loop(0, tT, step, tuple(h_s[n] for n in range(NB)))
        for n in range(NB):
            h_s[n] = hs[n]
            hl_ref[:, n * LRU_BLOCK_W:(n + 1) * LRU_BLOCK_W] = hs[n]

    @pl.when(j == 4)
    def _():
        for nc in range(nchunks):
            both_layouts(nc, project(nc), v_ref, v8_ref)
            for b in range(nc * (B // nchunks), (nc + 1) * (B // nchunks)):
                hseq = jnp.concatenate([b_s[n, b * pitch:b * pitch + tT, :] for n in range(NB)], axis=1)
                y = hseq * _gelu_tanh(gt_s[b * tT:(b + 1) * tT, :])
                yn_ref[b] = _rms(y, g_ref[...]).astype(BF16)


def _mixer_in(hmix, w_in, cw, cb, wax, ba, bx, lam, g, *, B, T, tT, width):
    D = hmix.shape[1]
    R = cw.shape[-1]
    H = width // HEAD_DIM
    NB = R // LRU_BLOCK_W
    ngroups = w_in.shape[1] // width
    assert ngroups == 5 and R == width
    const = lambda shp: pl.BlockSpec(shp, lambda t, j: (0,) * len(shp))
    slab = lambda w: pl.BlockSpec((B, tT, w), lambda t, j: (0, t, 0))
    hrows = pl.BlockSpec((B, tT * H, HEAD_DIM), lambda t, j: (0, t, 0))
    tokf = jax.ShapeDtypeStruct((B, T, width), F32)
    rowf = jax.ShapeDtypeStruct((B, T * H, HEAD_DIM), F32)
    q, k, v, k8, v8, yn, xt, hl = pl.pallas_call(
        functools.partial(_mixer_in_kernel, B=B, tT=tT),
        grid=(T // tT, ngroups),
        in_specs=[slab(D),
                  pl.BlockSpec((D, width), lambda t, j: (0, (j + 3) % 5)),
                  const(cw.shape), const(cb.shape), const(wax.shape), const(ba.shape),
                  const(bx.shape), const(lam.shape), const(g.shape)],
        out_specs=[slab(width), slab(width), slab(width), hrows, hrows, slab(R),
                   const((B, 8, R)), const((B, R))],
        out_shape=[tokf, tokf, tokf, rowf, rowf, jax.ShapeDtypeStruct((B, T, R), BF16),
                   jax.ShapeDtypeStruct((B, 8, R), F32), jax.ShapeDtypeStruct((B, R), F32)],
        scratch_shapes=[pltpu.VMEM((B, tT + 8, R), F32),
                        pltpu.VMEM((B * tT, R), F32),
                        pltpu.VMEM((B * tT, R), F32),
                        pltpu.VMEM((NB, B * _scan_pitch(tT), LRU_BLOCK_W), F32),
                        pltpu.VMEM((NB, B * _scan_pitch(tT), LRU_BLOCK_W), F32),
                        pltpu.VMEM((NB, B, LRU_BLOCK_W), F32)],
        compiler_params=_cparams(("arbitrary", "arbitrary")),
        name="mixer_in",
    )(hmix.reshape(B, T, D), w_in, cw, cb, wax, ba, bx, lam, g)
    M = B * T
    conv_state = xt[:, 8 - (CONV_WIDTH - 1):]
    return (q.reshape(M, width), k.reshape(M, width), v.reshape(M, width),
            k8.reshape(M * H, HEAD_DIM), v8.reshape(M * H, HEAD_DIM), yn.reshape(M, R), conv_state, hl)


def _lru_sample_kernel(xc_ref, gt_ref, h0_ref, cw_ref, cb_ref, wax_ref, ba_ref, bx_ref, lam_ref,
                       g_ref, yn_ref, hl_ref, *, B, T):
    NB = xc_ref.shape[0]
    L = CONV_WIDTH - 1 + T
    sp = _softplus(-lam_ref[...])
    hs = [h0_ref[:, n * LRU_BLOCK_W:(n + 1) * LRU_BLOCK_W] for n in range(NB)]
    for t in range(T):
        ys = []
        for n in range(NB):
            sl = slice(n * LRU_BLOCK_W, (n + 1) * LRU_BLOCK_W)
            un = cb_ref[:, sl]
            for j in range(CONV_WIDTH):
                un = un + cw_ref[j:j + 1, sl] * xc_ref[n, pl.ds(t + j, B, stride=L), :]
            a, bx = _lru_gate_act(_lru_gate_dot(un, n, wax_ref), un, n, ba_ref, bx_ref, sp)
            hs[n] = a * hs[n] + bx
            ys.append(hs[n] * _gelu_tanh(gt_ref[n, pl.ds(t, B, stride=T), :]))
        yn = _rms(jnp.concatenate(ys, axis=1), g_ref[...])
        for n in range(NB):
            yn_ref[n, pl.ds(t, B, stride=T), :] = yn[:, n * LRU_BLOCK_W:(n + 1) * LRU_BLOCK_W]
    for n in range(NB):
        hl_ref[:, n * LRU_BLOCK_W:(n + 1) * LRU_BLOCK_W] = hs[n]


def _lru_sample(xc, gate, h0, cw, cb, wax, ba, bx, lam, g, *, B, T):
    R = gate.shape[-1]
    NB = R // LRU_BLOCK_W
    split = lambda a: a.reshape(-1, NB, LRU_BLOCK_W).transpose(1, 0, 2)
    yn, hl = pl.pallas_call(
        functools.partial(_lru_sample_kernel, B=B, T=T),
        out_shape=[jax.ShapeDtypeStruct((NB, B * T, LRU_BLOCK_W), F32),
                   jax.ShapeDtypeStruct((B, R), F32)],
        compiler_params=pltpu.CompilerParams(vmem_limit_bytes=VMEM_LIMIT),
        name="lru_sample",
    )(split(xc), split(gate), h0, cw, cb, wax, ba, bx, lam, g)
    return yn.transpose(1, 0, 2).reshape(B * T, R), hl


def _attn_sample_kernel(q_ref, kn_ref, vn_ref, ck_ref, cv_ref, bias_ref, mult_ref, o_ref,
                        kb_s, vb_s):
    nbuf = ck_ref.shape[1]
    nnew = kn_ref.shape[1]
    npad = kb_s.shape[0] - nbuf - nnew
    kb_s[0:nbuf, :] = ck_ref[0].astype(BF16)
    vb_s[0:nbuf, :] = cv_ref[0].astype(BF16)
    kb_s[nbuf:nbuf + nnew, :] = kn_ref[0]
    vb_s[nbuf:nbuf + nnew, :] = vn_ref[0]
    kb_s[nbuf + nnew:, :] = jnp.zeros((npad, HEAD_DIM), BF16)
    vb_s[nbuf + nnew:, :] = jnp.zeros((npad, HEAD_DIM), BF16)

    s = lax.dot_general(q_ref[0], kb_s[...], (((1,), (1,)), ((), ())), preferred_element_type=F32)
    s = s + bias_ref[...]
    m = jnp.max(s, axis=-1, keepdims=True)
    p = jnp.exp(s - m) * mult_ref[...].astype(F32)
    l = jnp.sum(p, axis=-1, keepdims=True)
    pn = (p * (1.0 / l)).astype(BF16)
    o_ref[0] = jnp.dot(pn, vb_s[...], preferred_element_type=F32)


def _sample_tables(rel_bias, *, H, T, buf, nslots):
    slot = np.arange(nslots)[None, :]
    tq = np.arange(T)[:, None]
    dist = np.where(slot < buf, buf + tq - slot, tq - (slot - buf))
    real = (slot < buf + T) & (dist >= 0)
    mult = np.zeros((T, nslots), np.float32)
    for window, dil in DILATED_PATTERNS:
        mult += (real & (dist % dil == 0) & (dist // dil <= window // dil)).astype(np.float32)
    bucket = _rel_bucket(jnp.asarray(np.maximum(dist, 0), dtype=jnp.int32))
    bias = rel_bias[bucket].astype(F32).transpose(2, 0, 1)
    same_head = np.eye(H, dtype=bool)[:, None, None, :]
    keep = jnp.asarray(same_head & (mult > 0)[None, :, :, None])
    bias = jnp.where(keep, bias[..., None], NEG_INF).reshape(H * T, nslots * H)
    mult = np.broadcast_to(mult[None, :, :, None], (H, T, nslots, H)).reshape(H * T, nslots * H)
    return bias, jnp.asarray(mult, dtype=BF16)


def _attn_sample(q, k8, v8, cache_k, cache_v, rel_bias, *, B, T, H):
    buf = cache_k.shape[1]
    nslots = buf + 16
    qr = q.reshape(B, T, H, HEAD_DIM).transpose(0, 2, 1, 3).reshape(B, H * T, HEAD_DIM).astype(BF16)
    new = lambda a: a.reshape(B, T * H, HEAD_DIM).astype(BF16)
    bias, mult = _sample_tables(rel_bias, H=H, T=T, buf=buf, nslots=nslots)
    table = pl.BlockSpec((H * T, nslots * H), lambda b: (0, 0), pipeline_mode=pl.Buffered(1))
    per_b = lambda r: pl.BlockSpec((1, r, HEAD_DIM), lambda b: (b, 0, 0))
    o = pl.pallas_call(
        _attn_sample_kernel,
        grid=(B,),
        in_specs=[per_b(H * T), per_b(T * H), per_b(T * H), per_b(buf * H), per_b(buf * H), table, table],
        out_specs=per_b(H * T),
        out_shape=jax.ShapeDtypeStruct((B, H * T, HEAD_DIM), F32),
        scratch_shapes=[pltpu.VMEM((nslots * H, HEAD_DIM), BF16)] * 2,
        compiler_params=_cparams(("parallel",)),
        name="attn_sample",
    )(qr, new(k8), new(v8), cache_k.reshape(B, buf * H, HEAD_DIM), cache_v.reshape(B, buf * H, HEAD_DIM),
      bias, mult)
    return o.reshape(B, H, T, HEAD_DIM).transpose(0, 2, 1, 3).reshape(B * T, H * HEAD_DIM)


def kernel(x_prompt, x_sample, cache_k, cache_v, state_conv, state_h, g_ffn1, w1_gate, w1_up, w1_down, g_mix, w_in, conv_w, conv_b, w_a, b_a, w_x, b_x, lam, rel_bias, g_att_out, g_lru_out, w_out, g_ffn2, w2_gate, w2_up, w2_down, g_final):
    B, S, D = x_prompt.shape
    Bs, Ts, _ = x_sample.shape
    depth = g_ffn1.shape[0]
    assert depth == 1
    R = conv_w.shape[-1]
    H = rel_bias.shape[1]
    width = H * HEAD_DIM
    buf = cache_k.shape[2]
    l = 0

    row = lambda a: a.reshape(1, -1)
    bf = lambda a: a.astype(BF16)
    wg1, wu1, wd1 = bf(w1_gate[l]), bf(w1_up[l]), bf(w1_down[l])
    wg2, wu2, wd2 = bf(w2_gate[l]), bf(w2_up[l]), bf(w2_down[l])
    win, wout = bf(w_in[l]), bf(w_out[l])
    wax = bf(jnp.concatenate([w_a[l], w_x[l]], axis=-1))
    lru_w = (conv_w[l], row(conv_b[l]), wax, row(b_a[l]), row(b_x[l]), row(lam[l]), row(g_lru_out[l]))
    tf = 512

    def ffn_in(x2d, tm):
        return _ffn_in(x2d, row(g_ffn1[l]), wg1, wu1, wd1, row(g_mix[l]), tm=tm, tf=tf)

    def trunk_out(x1, oatt, yn, tm):
        return _ffn_out(x1, oatt, yn, row(g_att_out[l]), wout, row(g_ffn2[l]), wg2, wu2, wd2,
                        row(g_final), tm=tm, tf=tf)

    Mp = B * S
    x1, hmix = ffn_in(x_prompt.reshape(Mp, D), 512)
    q, k, v, k8, v8, yn, conv_p, h_p = _mixer_in(hmix, win, *lru_w, B=B, T=S, tT=32, width=width)
    dils = tuple(d for _, d in DILATED_PATTERNS)
    bias_q = jnp.stack([_band_bias_t(rel_bias, w, d) for w, d in DILATED_PATTERNS])
    bias_q = jnp.swapaxes(bias_q, -1, -2) * LOG2E
    oatt = _attn_prompt(q, k, v, bias_q, B=B, S=S, H=H, dils=dils)
    y_prompt = trunk_out(x1, oatt, yn, 512).reshape(B, S, D)
    keep = min(MAX_WINDOW, S)
    k_prompt = k8.reshape(B, S, H, HEAD_DIM)[:, S - keep:][None]
    v_prompt = v8.reshape(B, S, H, HEAD_DIM)[:, S - keep:][None]
    conv_prompt = conv_p[None]

    Ms = Bs * Ts
    x1s, hmixs = ffn_in(x_sample.reshape(Ms, D), Ms)
    qs, _, _, k8s, v8s, xls, gates = _inproj(hmixs, win, tm=Ms, width=width)
    o_s = _attn_sample(qs, k8s, v8s, cache_k[l], cache_v[l], rel_bias, B=Bs, T=Ts, H=H)
    xc = jnp.concatenate([state_conv[l], xls.reshape(Bs, Ts, R)], axis=1)
    yns, h_s = _lru_sample(xc, gates, state_h[l], *lru_w, B=Bs, T=Ts)
    y_sample = trunk_out(x1s, o_s, yns, Ms).reshape(Bs, Ts, D)
    k_sample = k8s.reshape(Bs, Ts, H, HEAD_DIM)[None]
    v_sample = v8s.reshape(Bs, Ts, H, HEAD_DIM)[None]
    conv_sample = xc[:, -(CONV_WIDTH - 1):][None]

    return (y_prompt, y_sample, k_prompt, v_prompt, conv_prompt, h_p[None],
            k_sample, v_sample, conv_sample, h_s[None])
```

```python
import functools
import math

import jax
import jax.numpy as jnp
import numpy as np
from jax import lax
from jax.experimental import pallas as pl
from jax.experimental.pallas import tpu as pltpu

F32 = jnp.float32
BF16 = jnp.bfloat16

HEAD_DIM = 128
LRU_BLOCK_W = 128
CONV_WIDTH = 4
LRU_C = 8.0
DILATED_PATTERNS = ((128, 1), (512, 4), (2048, 16))
MAX_WINDOW = 2048
N_BUCKETS = 32
MAX_DISTANCE = MAX_WINDOW
RMS_EPS = 1e-6
NEG_INF = -1e30
ATT_SCALE = 1.0 / math.sqrt(HEAD_DIM)

LANE = 128
MIB = 1024 * 1024
VMEM_LIMIT = 56 * MIB


def _cparams(semantics):
    return pltpu.CompilerParams(dimension_semantics=semantics, vmem_limit_bytes=VMEM_LIMIT)


def _rms(x, g):
    ms = jnp.mean(x * x, axis=-1, keepdims=True)
    return (x * lax.rsqrt(ms + RMS_EPS)) * g


def _sigmoid(x):
    return 1.0 / (1.0 + jnp.exp(-x))


def _gelu_tanh(x):
    c = math.sqrt(2.0 / math.pi)
    return x * (0.5 * (1.0 + jnp.tanh(c * (x + 0.044715 * (x * x * x)))))


def _softplus(z):
    return jnp.maximum(z, 0.0) + jnp.log1p(jnp.exp(-jnp.abs(z)))


def _neg_expm1_2x(x):
    t = jnp.tanh(x)
    return (-2.0 * t) / (1.0 - t)


CAST_BLOCK_BYTES = 6 * MIB


def _cast_kernel(w_ref, o_ref):
    o_ref[...] = w_ref[...].astype(o_ref.dtype)


def _to_bf16(w):
    rows, cols = w.shape
    rb = rows
    while rb * cols * w.dtype.itemsize > CAST_BLOCK_BYTES and rb % 32 == 0:
        rb //= 2
    return pl.pallas_call(
        _cast_kernel,
        grid=(rows // rb,),
        in_specs=[pl.BlockSpec((rb, cols), lambda i: (i, 0))],
        out_specs=pl.BlockSpec((rb, cols), lambda i: (i, 0)),
        out_shape=jax.ShapeDtypeStruct((rows, cols), BF16),
        compiler_params=_cparams(("parallel",)),
        name="cast_bf16",
    )(w)


def _swiglu_accumulate(hn_ref, wg_ref, wu_ref, wd_ref, o_ref):
    hn = hn_ref[...]
    a = jnp.dot(hn, wg_ref[...], preferred_element_type=F32)
    b = jnp.dot(hn, wu_ref[...], preferred_element_type=F32)
    act = ((a * _sigmoid(a)) * b).astype(BF16)
    o_ref[...] += jnp.dot(act, wd_ref[...], preferred_element_type=F32)


def _ffn_in_kernel(x_ref, g_ref, wg_ref, wu_ref, wd_ref, g2_ref, o_ref, h2_ref, hn_ref, *, nj):
    j = pl.program_id(1)

    @pl.when(j == 0)
    def _():
        hn_ref[...] = _rms(x_ref[...], g_ref[...]).astype(BF16)
        o_ref[...] = jnp.zeros_like(o_ref)

    _swiglu_accumulate(hn_ref, wg_ref, wu_ref, wd_ref, o_ref)

    @pl.when(j == nj - 1)
    def _():
        xn = x_ref[...] + 0.5 * o_ref[...]
        o_ref[...] = xn
        h2_ref[...] = _rms(xn, g2_ref[...]).astype(BF16)


def _ffn_in(x, g, wg, wu, wd, g2, *, tm, tf):
    M, D = x.shape
    nj = wg.shape[1] // tf
    tok = pl.BlockSpec((tm, D), lambda i, j: (i, 0))
    vec = pl.BlockSpec((1, D), lambda i, j: (0, 0))
    return pl.pallas_call(
        functools.partial(_ffn_in_kernel, nj=nj),
        grid=(M // tm, nj),
        in_specs=[tok, vec,
                  pl.BlockSpec((D, tf), lambda i, j: (0, j)),
                  pl.BlockSpec((D, tf), lambda i, j: (0, j)),
                  pl.BlockSpec((tf, D), lambda i, j: (j, 0)),
                  vec],
        out_specs=[tok, tok],
        out_shape=[jax.ShapeDtypeStruct((M, D), F32), jax.ShapeDtypeStruct((M, D), BF16)],
        scratch_shapes=[pltpu.VMEM((tm, D), BF16)],
        compiler_params=_cparams(("parallel", "arbitrary")),
        name="ffn_in",
    )(x, g, wg, wu, wd, g2)


def _ffn_out_kernel(x_ref, oa_ref, yn_ref, ga_ref, wo_ref, g_ref, wg_ref, wu_ref, wd_ref, g2_ref,
                    o_ref, hn_ref, x2_ref, *, nj):
    j = pl.program_id(1)

    @pl.when(j == 0)
    def _():
        aw = oa_ref.shape[1]
        an = _rms(oa_ref[...], ga_ref[...]).astype(BF16)
        proj = jnp.dot(an, wo_ref[0:aw, :], preferred_element_type=F32)
        proj = proj + jnp.dot(yn_ref[...].astype(BF16), wo_ref[aw:, :], preferred_element_type=F32)
        x2 = x_ref[...] + proj
        x2_ref[...] = x2
        hn_ref[...] = _rms(x2, g_ref[...]).astype(BF16)
        o_ref[...] = jnp.zeros_like(o_ref)

    _swiglu_accumulate(hn_ref, wg_ref, wu_ref, wd_ref, o_ref)

    @pl.when(j == nj - 1)
    def _():
        o_ref[...] = _rms(x2_ref[...] + 0.5 * o_ref[...], g2_ref[...])


def _ffn_out(x, oatt, yn, g_att, w_out, g, wg, wu, wd, g2, *, tm, tf):
    M, D = x.shape
    nj = wg.shape[1] // tf
    tok = lambda w: pl.BlockSpec((tm, w), lambda i, j: (i, 0))
    vec = lambda w: pl.BlockSpec((1, w), lambda i, j: (0, 0))
    return pl.pallas_call(
        functools.partial(_ffn_out_kernel, nj=nj),
        grid=(M // tm, nj),
        in_specs=[tok(D), tok(oatt.shape[1]), tok(yn.shape[1]), vec(oatt.shape[1]),
                  pl.BlockSpec(w_out.shape, lambda i, j: (0, 0), pipeline_mode=pl.Buffered(1)),
                  vec(D),
                  pl.BlockSpec((D, tf), lambda i, j: (0, j)),
                  pl.BlockSpec((D, tf), lambda i, j: (0, j)),
                  pl.BlockSpec((tf, D), lambda i, j: (j, 0)),
                  vec(D)],
        out_specs=tok(D),
        out_shape=jax.ShapeDtypeStruct((M, D), F32),
        scratch_shapes=[pltpu.VMEM((tm, D), BF16), pltpu.VMEM((tm, D), F32)],
        compiler_params=_cparams(("parallel", "arbitrary")),
        name="ffn_out",
    )(x, oatt, yn, g_att, w_out, g, wg, wu, wd, g2)


def _inproj_kernel(h_ref, w_ref, q_ref, k_ref, v_ref, k8_ref, v8_ref, xl_ref, gt_ref):
    j = pl.program_id(1)
    tm, width = k_ref.shape
    H = width // HEAD_DIM

    def project():
        return jnp.dot(h_ref[...], w_ref[...], preferred_element_type=F32)

    def both_layouts(tok_ref, rows_ref):
        r = project()
        tok_ref[...] = r
        for h in range(H):
            rows_ref[pl.ds(h, tm, stride=H), :] = r[:, h * HEAD_DIM:(h + 1) * HEAD_DIM]

    @pl.when(j == 0)
    def _():
        q_ref[...] = project() * ATT_SCALE

    @pl.when(j == 1)
    def _():
        both_layouts(k_ref, k8_ref)

    @pl.when(j == 2)
    def _():
        both_layouts(v_ref, v8_ref)

    @pl.when(j == 3)
    def _():
        xl_ref[...] = project()

    @pl.when(j == 4)
    def _():
        gt_ref[...] = project()


def _inproj(h, w_in, *, tm, width):
    M, D = h.shape
    assert w_in.shape[1] == 5 * width
    H = width // HEAD_DIM
    ospec = pl.BlockSpec((tm, width), lambda i, j: (i, 0))
    hspec = pl.BlockSpec((tm * H, HEAD_DIM), lambda i, j: (i, 0))
    tok = jax.ShapeDtypeStruct((M, width), F32)
    rows = jax.ShapeDtypeStruct((M * H, HEAD_DIM), F32)
    return pl.pallas_call(
        _inproj_kernel,
        grid=(M // tm, 5),
        in_specs=[
            pl.BlockSpec((tm, D), lambda i, j: (i, 0)),
            pl.BlockSpec((D, width), lambda i, j: (0, j)),
        ],
        out_specs=[ospec, ospec, ospec, hspec, hspec, ospec, ospec],
        out_shape=[tok, tok, tok, rows, rows, tok, tok],
        compiler_params=_cparams(("parallel", "arbitrary")),
        name="inproj",
    )(h, w_in)


def _rel_bucket(dist):
    max_exact = N_BUCKETS // 2
    df = jnp.maximum(dist, 1).astype(F32)
    large = max_exact + (jnp.log(df / max_exact) / math.log(MAX_DISTANCE / max_exact)
                         * (N_BUCKETS - max_exact)).astype(jnp.int32)
    return jnp.where(dist < max_exact, dist, jnp.minimum(large, N_BUCKETS - 1))


def _band_bias_t(rel_bias, window, dil):
    n = window // dil
    assert n == LANE
    bias_j = rel_bias[_rel_bucket(dil * jnp.arange(n + 1, dtype=jnp.int32))].T.astype(F32)
    H = bias_j.shape[0]
    neg = jnp.full((H, n - 1), NEG_INF, F32)
    lg = 3 * n - 1
    wext = jnp.concatenate([neg, bias_j, neg, jnp.zeros((H, 1), F32)], axis=1)
    skew = jnp.tile(wext, (1, 2 * n))[:, :2 * n * lg].reshape(H, 2 * n, lg)
    return skew[:, :, 2 * n - 1:2 * n - 1 + n]


LOG2E = math.log2(math.e)
MIX_ROWS = 256


def _attn_kernel(q_ref, k_ref, v_ref, bias_ref, o_ref, qc_s, kc_s, vc_s, num_s, den_s, max_s, *, S, dils):
    npat = len(dils)
    nt = (((1,), (1,)), ((), ()))

    def rows(first, count, d):
        return pl.ds(first, count) if d == 1 else pl.ds(first, count, stride=d)

    for p, d in enumerate(dils):
        n = S // d
        vc_s[p, :, HEAD_DIM:] = jnp.ones((S, HEAD_DIM), BF16)
        for r in range(d):
            src = rows(r, n, d)
            dst = slice(r * n, (r + 1) * n)
            qc_s[p, dst, :] = q_ref[0, src, :].astype(BF16)
            kc_s[p, dst, :] = k_ref[0, src, :].astype(BF16)
            vc_s[p, dst, 0:HEAD_DIM] = v_ref[0, src, :].astype(BF16)

    for p, d in enumerate(dils):
        nblk = S // (d * LANE)
        for r in range(d):
            for c in range(nblk):
                u = r * nblk + c
                lo = u if c == 0 else u - 1
                q = qc_s[p, u * LANE:(u + 1) * LANE, :]
                k2 = kc_s[p, lo * LANE:(u + 1) * LANE, :]
                v2 = vc_s[p, lo * LANE:(u + 1) * LANE, :]
                bias = bias_ref[p, 0, :, LANE:] if c == 0 else bias_ref[p, 0]
                s2 = lax.dot_general(q, k2, nt, preferred_element_type=F32) + bias
                m2 = jnp.max(s2, axis=1, keepdims=True)
                e = jnp.exp2(s2 - m2).astype(BF16)
                oe = jnp.dot(e, v2, preferred_element_type=F32)
                prow = rows(r + d * c * LANE, LANE, d)
                num_s[p, prow, :] = oe[:, :HEAD_DIM]
                den_s[p, prow, :] = oe[:, HEAD_DIM:]
                max_s[p, prow, :] = jnp.broadcast_to(m2, (LANE, HEAD_DIM))

    for ch in range(S // MIX_ROWS):
        sl = slice(ch * MIX_ROWS, (ch + 1) * MIX_ROWS)
        ms = [max_s[p, sl, :] for p in range(npat)]
        top = functools.reduce(jnp.maximum, ms)
        es = [jnp.exp2(m - top) for m in ms]
        num = es[0] * num_s[0, sl, :]
        den = es[0] * den_s[0, sl, :]
        for p in range(1, npat):
            num = num + es[p] * num_s[p, sl, :]
            den = den + es[p] * den_s[p, sl, :]
        o_ref[0, sl, :] = num * (1.0 / den)


def _attn_prompt(q, k, v, bias_t, *, B, S, H, dils):
    npat = len(dils)
    width = H * HEAD_DIM
    col = pl.BlockSpec((1, S, HEAD_DIM), lambda b, h: (b, 0, h))
    q3, k3, v3 = (a.reshape(B, S, width) for a in (q, k, v))
    o = pl.pallas_call(
        functools.partial(_attn_kernel, S=S, dils=dils),
        grid=(B, H),
        in_specs=[col, col, col,
                  pl.BlockSpec((npat, 1, LANE, 2 * LANE), lambda b, h: (0, h, 0, 0))],
        out_specs=col,
        out_shape=jax.ShapeDtypeStruct((B, S, width), F32),
        scratch_shapes=[pltpu.VMEM((npat, S, HEAD_DIM), BF16), pltpu.VMEM((npat, S, HEAD_DIM), BF16),
                        pltpu.VMEM((npat, S, 2 * HEAD_DIM), BF16),
                        pltpu.VMEM((npat, S, HEAD_DIM), F32), pltpu.VMEM((npat, S, HEAD_DIM), F32),
                        pltpu.VMEM((npat, S, HEAD_DIM), F32)],
        compiler_params=_cparams(("parallel", "parallel")),
        name="attn_prompt",
    )(q3, k3, v3, bias_t)
    return o.reshape(B * S, width)


def _scan_pitch(rows):
    pitch = -(-rows // 8) * 8
    return pitch if (pitch // 8) % 2 else pitch + 8


def _lru_gate_dot(un, n, wax_ref):
    return jnp.dot(un.astype(BF16), wax_ref[n], preferred_element_type=F32)


def _lru_gate_act(gx, un, n, ba_ref, bx_ref, sp):
    sl = slice(n * LRU_BLOCK_W, (n + 1) * LRU_BLOCK_W)
    r = _sigmoid(gx[:, :LRU_BLOCK_W] + ba_ref[:, sl])
    i = _sigmoid(gx[:, LRU_BLOCK_W:] + bx_ref[:, sl])
    log_a = (-LRU_C * r) * sp[:, sl]
    z = _neg_expm1_2x(log_a)
    root = jnp.where(z > 0.0, z * lax.rsqrt(z), 0.0)
    return jnp.exp(log_a), root * (i * un)


def _mixer_in_kernel(h_ref, w_ref, cw_ref, cb_ref, wax_ref, ba_ref, bx_ref, lam_ref, g_ref,
                     q_ref, k_ref, v_ref, k8_ref, v8_ref, yn_ref, xt_ref, hl_ref,
                     xc_s, gt_s, u_s, a_s, b_s, h_s, *, B, tT):
    t = pl.program_id(0)
    j = pl.program_id(1)
    R = xc_s.shape[-1]
    NB = R // LRU_BLOCK_W
    H = k_ref.shape[-1] // HEAD_DIM
    pad = 8
    pitch = a_s.shape[1] // B

    CW = 2 * LRU_BLOCK_W
    nchunks = R // CW

    def project(nc):
        h2 = h_ref[...].reshape(B * tT, h_ref.shape[-1])
        return jnp.dot(h2, w_ref[:, nc * CW:(nc + 1) * CW], preferred_element_type=F32)

    def both_layouts(nc, r, tok_ref, rows_ref):
        tok_ref[:, :, nc * CW:(nc + 1) * CW] = r.reshape(B, tT, CW)
        for hh in range(CW // HEAD_DIM):
            h = nc * (CW // HEAD_DIM) + hh
            for b in range(B):
                rows_ref[b, pl.ds(h, tT, stride=H), :] = r[b * tT:(b + 1) * tT, hh * HEAD_DIM:(hh + 1) * HEAD_DIM]

    @pl.when(jnp.logical_and(t == 0, j == 0))
    def _():
        xc_s[:, 0:pad, :] = jnp.zeros((B, pad, R), F32)
        h_s[...] = jnp.zeros_like(h_s)

    @pl.when(j == 0)
    def _():
        for nc in range(nchunks):
            xc_s[:, pad:, nc * CW:(nc + 1) * CW] = project(nc).reshape(B, tT, CW)

    @pl.when(j == 1)
    def _():
        for nc in range(nchunks):
            cs = slice(nc * CW, (nc + 1) * CW)
            gt_s[:, cs] = project(nc)
            for b in range(B):
                u = cb_ref[:, cs]
                for c in range(CONV_WIDTH):
                    off = pad - (CONV_WIDTH - 1) + c
                    u = u + cw_ref[c:c + 1, cs] * xc_s[b, off:off + tT, cs]
                u_s[b * tT:(b + 1) * tT, cs] = u
        xt_ref[...] = xc_s[:, tT:tT + pad, :]
        xc_s[:, pad - (CONV_WIDTH - 1):pad, :] = xc_s[:, pad + tT - (CONV_WIDTH - 1):pad + tT, :]

    @pl.when(j == 2)
    def _():
        sp = _softplus(-lam_ref[...])
        for nc in range(nchunks):
            q_ref[:, :, nc * CW:(nc + 1) * CW] = (project(nc) * (ATT_SCALE * LOG2E)).reshape(B, tT, CW)
            for n in range(nc * (CW // LRU_BLOCK_W), (nc + 1) * (CW // LRU_BLOCK_W)):
                un = u_s[:, n * LRU_BLOCK_W:(n + 1) * LRU_BLOCK_W]
                gx = _lru_gate_dot(un, n, wax_ref)
                for b in range(B):
                    rows = slice(b * tT, (b + 1) * tT)
                    a, bx = _lru_gate_act(gx[rows], un[rows], n, ba_ref, bx_ref, sp)
                    a_s[n, b * pitch:b * pitch + tT, :] = a
                    b_s[n, b * pitch:b * pitch + tT, :] = bx

    @pl.when(j == 3)
    def _():
        for nc in range(nchunks):
            both_layouts(nc, project(nc), k_ref, k8_ref)

        def step(tt, hs):
            rows = pl.ds(tt, B, stride=pitch)
            new = []
            for n in range(NB):
                hn = a_s[n, rows, :] * hs[n] + b_s[n, rows, :]
                b_s[n, rows, :] = hn
                new.append(hn)
            return tuple(new)

        hs = lax.fori_loop(0, tT, step, tuple(h_s[n] for n in range(NB)))
        for n in range(NB):
            h_s[n] = hs[n]
            hl_ref[:, n * LRU_BLOCK_W:(n + 1) * LRU_BLOCK_W] = hs[n]

    @pl.when(j == 4)
    def _():
        for nc in range(nchunks):
            both_layouts(nc, project(nc), v_ref, v8_ref)
            for b in range(nc * (B // nchunks), (nc + 1) * (B // nchunks)):
                hseq = jnp.concatenate([b_s[n, b * pitch:b * pitch + tT, :] for n in range(NB)], axis=1)
                y = hseq * _gelu_tanh(gt_s[b * tT:(b + 1) * tT, :])
                yn_ref[b] = _rms(y, g_ref[...]).astype(BF16)


def _mixer_in(hmix, w_in, cw, cb, wax, ba, bx, lam, g, *, B, T, tT, width):
    D = hmix.shape[1]
    R = cw.shape[-1]
    H = width // HEAD_DIM
    NB = R // LRU_BLOCK_W
    ngroups = w_in.shape[1] // width
    assert ngroups == 5 and R == width
    const = lambda shp: pl.BlockSpec(shp, lambda t, j: (0,) * len(shp))
    slab = lambda w: pl.BlockSpec((B, tT, w), lambda t, j: (0, t, 0))
    hrows = pl.BlockSpec((B, tT * H, HEAD_DIM), lambda t, j: (0, t, 0))
    tokf = jax.ShapeDtypeStruct((B, T, width), F32)
    rowf = jax.ShapeDtypeStruct((B, T * H, HEAD_DIM), F32)
    q, k, v, k8, v8, yn, xt, hl = pl.pallas_call(
        functools.partial(_mixer_in_kernel, B=B, tT=tT),
        grid=(T // tT, ngroups),
        in_specs=[slab(D),
                  pl.BlockSpec((D, width), lambda t, j: (0, (j + 3) % 5)),
                  const(cw.shape), const(cb.shape), const(wax.shape), const(ba.shape),
                  const(bx.shape), const(lam.shape), const(g.shape)],
        out_specs=[slab(width), slab(width), slab(width), hrows, hrows, slab(R),
                   const((B, 8, R)), const((B, R))],
        out_shape=[tokf, tokf, tokf, rowf, rowf, jax.ShapeDtypeStruct((B, T, R), BF16),
                   jax.ShapeDtypeStruct((B, 8, R), F32), jax.ShapeDtypeStruct((B, R), F32)],
        scratch_shapes=[pltpu.VMEM((B, tT + 8, R), F32),
                        pltpu.VMEM((B * tT, R), F32),
                        pltpu.VMEM((B * tT, R), F32),
                        pltpu.VMEM((NB, B * _scan_pitch(tT), LRU_BLOCK_W), F32),
                        pltpu.VMEM((NB, B * _scan_pitch(tT), LRU_BLOCK_W), F32),
                        pltpu.VMEM((NB, B, LRU_BLOCK_W), F32)],
        compiler_params=_cparams(("arbitrary", "arbitrary")),
        name="mixer_in",
    )(hmix.reshape(B, T, D), w_in, cw, cb, wax, ba, bx, lam, g)
    M = B * T
    conv_state = xt[:, 8 - (CONV_WIDTH - 1):]
    return (q.reshape(M, width), k.reshape(M, width), v.reshape(M, width),
            k8.reshape(M * H, HEAD_DIM), v8.reshape(M * H, HEAD_DIM), yn.reshape(M, R), conv_state, hl)


def _lru_sample_kernel(xc_ref, gt_ref, h0_ref, cw_ref, cb_ref, wax_ref, ba_ref, bx_ref, lam_ref,
                       g_ref, yn_ref, hl_ref, *, B, T):
    NB = xc_ref.shape[0]
    L = CONV_WIDTH - 1 + T
    sp = _softplus(-lam_ref[...])
    hs = [h0_ref[:, n * LRU_BLOCK_W:(n + 1) * LRU_BLOCK_W] for n in range(NB)]
    for t in range(T):
        ys = []
        for n in range(NB):
            sl = slice(n * LRU_BLOCK_W, (n + 1) * LRU_BLOCK_W)
            un = cb_ref[:, sl]
            for j in range(CONV_WIDTH):
                un = un + cw_ref[j:j + 1, sl] * xc_ref[n, pl.ds(t + j, B, stride=L), :]
            a, bx = _lru_gate_act(_lru_gate_dot(un, n, wax_ref), un, n, ba_ref, bx_ref, sp)
            hs[n] = a * hs[n] + bx
            ys.append(hs[n] * _gelu_tanh(gt_ref[n, pl.ds(t, B, stride=T), :]))
        yn = _rms(jnp.concatenate(ys, axis=1), g_ref[...])
        for n in range(NB):
            yn_ref[n, pl.ds(t, B, stride=T), :] = yn[:, n * LRU_BLOCK_W:(n + 1) * LRU_BLOCK_W]
    for n in range(NB):
        hl_ref[:, n * LRU_BLOCK_W:(n + 1) * LRU_BLOCK_W] = hs[n]


def _lru_sample(xc, gate, h0, cw, cb, wax, ba, bx, lam, g, *, B, T):
    R = gate.shape[-1]
    NB = R // LRU_BLOCK_W
    split = lambda a: a.reshape(-1, NB, LRU_BLOCK_W).transpose(1, 0, 2)
    yn, hl = pl.pallas_call(
        functools.partial(_lru_sample_kernel, B=B, T=T),
        out_shape=[jax.ShapeDtypeStruct((NB, B * T, LRU_BLOCK_W), F32),
                   jax.ShapeDtypeStruct((B, R), F32)],
        compiler_params=pltpu.CompilerParams(vmem_limit_bytes=VMEM_LIMIT),
        name="lru_sample",
    )(split(xc), split(gate), h0, cw, cb, wax, ba, bx, lam, g)
    return yn.transpose(1, 0, 2).reshape(B * T, R), hl


def _attn_sample_kernel(q_ref, kn_ref, vn_ref, cka_ref, ckb_ref, cva_ref, cvb_ref, bias_ref, mult_ref,
                        o_ref, kb_s, vb_s):
    na = cka_ref.shape[1] * cka_ref.shape[2]
    nb = ckb_ref.shape[1]
    nnew = kn_ref.shape[1]
    npad = kb_s.shape[0] - na - nb - nnew
    for dst, a_ref, b_ref, n_ref in ((kb_s, cka_ref, ckb_ref, kn_ref), (vb_s, cva_ref, cvb_ref, vn_ref)):
        dst[0:na, :] = a_ref[0].reshape(na, HEAD_DIM).astype(BF16)
        dst[na:na + nb, :] = b_ref[0].astype(BF16)
        dst[na + nb:na + nb + nnew, :] = n_ref[0]
        dst[na + nb + nnew:, :] = jnp.zeros((npad, HEAD_DIM), BF16)

    s = lax.dot_general(q_ref[0], kb_s[...], (((1,), (1,)), ((), ())), preferred_element_type=F32)
    s = s + bias_ref[...]
    m = jnp.max(s, axis=-1, keepdims=True)
    p = jnp.exp(s - m) * mult_ref[...].astype(F32)
    l = jnp.sum(p, axis=-1, keepdims=True)
    pn = (p * (1.0 / l)).astype(BF16)
    o_ref[0] = jnp.dot(pn, vb_s[...], preferred_element_type=F32)


def _sample_key_rows(*, H, T, buf):
    maxdil = max(d for _, d in DILATED_PATTERNS)
    tail = max(w for w, d in DILATED_PATTERNS if d != maxdil)
    assert buf % maxdil == 0 and tail % maxdil == 0 and buf % tail == 0 and T <= maxdil
    assert (maxdil * H) % 8 == 0 and (T * H) % 16 == 0
    ngroups = (buf - tail) // maxdil
    g, r, h = np.meshgrid(np.arange(ngroups), np.arange(T), np.arange(H), indexing="ij")
    pos_a, head_a = (maxdil * g + r).ravel(), h.ravel()
    t, h = np.meshgrid(np.arange(tail), np.arange(H), indexing="ij")
    pos_b, head_b = (buf - tail + t).ravel(), h.ravel()
    i, h = np.meshgrid(np.arange(T), np.arange(H), indexing="ij")
    pos_n, head_n = (buf + i).ravel(), h.ravel()
    nreal = pos_a.size + pos_b.size + pos_n.size
    nrows = -(-nreal // LANE) * LANE
    pos = np.concatenate([pos_a, pos_b, pos_n, np.zeros(nrows - nreal, np.int64)])
    head = np.concatenate([head_a, head_b, head_n, np.zeros(nrows - nreal, np.int64)])
    real = np.arange(nrows) < nreal
    return dict(maxdil=maxdil, tail=tail, ngroups=ngroups, nrows=nrows, pos=pos, head=head, real=real)


def _sample_tables(rel_bias, rows, *, H, T, buf):
    dist = buf + np.arange(T)[:, None] - rows["pos"][None, :]
    ok = rows["real"][None, :] & (dist >= 0)
    mult = np.zeros(dist.shape, np.float32)
    for window, dil in DILATED_PATTERNS:
        mult += (ok & (dist % dil == 0) & (dist // dil <= window // dil)).astype(np.float32)
    bucket = _rel_bucket(jnp.asarray(np.maximum(dist, 0), dtype=jnp.int32))
    bias = rel_bias[bucket].astype(F32).transpose(2, 0, 1)
    same_head = np.arange(H)[:, None, None] == rows["head"][None, None, :]
    keep = same_head & (mult > 0)[None]
    bias = jnp.where(jnp.asarray(keep), bias, NEG_INF).reshape(H * T, -1)
    mult = np.broadcast_to(mult[None], keep.shape).reshape(H * T, -1)
    return bias, jnp.asarray(mult, dtype=BF16)


def _attn_sample(q, k8, v8, cache_k, cache_v, rel_bias, *, B, T, H):
    buf = cache_k.shape[1]
    rows = _sample_key_rows(H=H, T=T, buf=buf)
    maxdil, tail, ngroups, nrows = rows["maxdil"], rows["tail"], rows["ngroups"], rows["nrows"]
    qr = q.reshape(B, T, H, HEAD_DIM).transpose(0, 2, 1, 3).reshape(B, H * T, HEAD_DIM).astype(BF16)
    new = lambda a: a.reshape(B, T * H, HEAD_DIM).astype(BF16)
    bias, mult = _sample_tables(rel_bias, rows, H=H, T=T, buf=buf)
    table = pl.BlockSpec((H * T, nrows), lambda b: (0, 0))
    per_b = lambda r: pl.BlockSpec((1, r, HEAD_DIM), lambda b: (b, 0, 0))
    old = pl.BlockSpec((1, ngroups, T * H, HEAD_DIM), lambda b: (b, 0, 0, 0))
    grouped = lambda c: c.reshape(B, buf // maxdil, maxdil * H, HEAD_DIM)
    last = pl.BlockSpec((1, tail * H, HEAD_DIM), lambda b: (b, buf // tail - 1, 0))
    flat = lambda c: c.reshape(B, buf * H, HEAD_DIM)
    o = pl.pallas_call(
        _attn_sample_kernel,
        grid=(B,),
        in_specs=[per_b(H * T), per_b(T * H), per_b(T * H), old, last, old, last, table, table],
        out_specs=per_b(H * T),
        out_shape=jax.ShapeDtypeStruct((B, H * T, HEAD_DIM), F32),
        scratch_shapes=[pltpu.VMEM((nrows, HEAD_DIM), BF16)] * 2,
        compiler_params=_cparams(("parallel",)),
        name="attn_sample",
    )(qr, new(k8), new(v8), grouped(cache_k), flat(cache_k), grouped(cache_v), flat(cache_v), bias, mult)
    return o.reshape(B, H, T, HEAD_DIM).transpose(0, 2, 1, 3).reshape(B * T, H * HEAD_DIM)


def kernel(x_prompt, x_sample, cache_k, cache_v, state_conv, state_h, g_ffn1, w1_gate, w1_up, w1_down, g_mix, w_in, conv_w, conv_b, w_a, b_a, w_x, b_x, lam, rel_bias, g_att_out, g_lru_out, w_out, g_ffn2, w2_gate, w2_up, w2_down, g_final):
    B, S, D = x_prompt.shape
    Bs, Ts, _ = x_sample.shape
    depth = g_ffn1.shape[0]
    assert depth == 1
    R = conv_w.shape[-1]
    H = rel_bias.shape[1]
    width = H * HEAD_DIM
    buf = cache_k.shape[2]
    l = 0

    row = lambda a: a.reshape(1, -1)
    wg1, wu1, wd1 = _to_bf16(w1_gate[l]), _to_bf16(w1_up[l]), _to_bf16(w1_down[l])
    wg2, wu2, wd2 = _to_bf16(w2_gate[l]), _to_bf16(w2_up[l]), _to_bf16(w2_down[l])
    win, wout = _to_bf16(w_in[l]), _to_bf16(w_out[l])
    wax = jnp.concatenate([w_a[l], w_x[l]], axis=-1).astype(BF16)
    lru_w = (conv_w[l], row(conv_b[l]), wax, row(b_a[l]), row(b_x[l]), row(lam[l]), row(g_lru_out[l]))
    tf = 512

    def ffn_in(x2d, tm):
        return _ffn_in(x2d, row(g_ffn1[l]), wg1, wu1, wd1, row(g_mix[l]), tm=tm, tf=tf)

    def trunk_out(x1, oatt, yn, tm):
        return _ffn_out(x1, oatt, yn, row(g_att_out[l]), wout, row(g_ffn2[l]), wg2, wu2, wd2,
                        row(g_final), tm=tm, tf=tf)

    Mp = B * S
    x1, hmix = ffn_in(x_prompt.reshape(Mp, D), 512)
    q, k, v, k8, v8, yn, conv_p, h_p = _mixer_in(hmix, win, *lru_w, B=B, T=S, tT=32, width=width)
    dils = tuple(d for _, d in DILATED_PATTERNS)
    bias_q = jnp.stack([_band_bias_t(rel_bias, w, d) for w, d in DILATED_PATTERNS])
    bias_q = jnp.swapaxes(bias_q, -1, -2) * LOG2E
    oatt = _attn_prompt(q, k, v, bias_q, B=B, S=S, H=H, dils=dils)
    y_prompt = trunk_out(x1, oatt, yn, 512).reshape(B, S, D)
    keep = min(MAX_WINDOW, S)
    k_prompt = k8.reshape(B, S, H, HEAD_DIM)[:, S - keep:][None]
    v_prompt = v8.reshape(B, S, H, HEAD_DIM)[:, S - keep:][None]
    conv_prompt = conv_p[None]

    Ms = Bs * Ts
    x1s, hmixs = ffn_in(x_sample.reshape(Ms, D), Ms)
    qs, _, _, k8s, v8s, xls, gates = _inproj(hmixs, win, tm=Ms, width=width)
    o_s = _attn_sample(qs, k8s, v8s, cache_k[l], cache_v[l], rel_bias, B=Bs, T=Ts, H=H)
    xc = jnp.concatenate([state_conv[l], xls.reshape(Bs, Ts, R)], axis=1)
    yns, h_s = _lru_sample(xc, gates, state_h[l], *lru_w, B=Bs, T=Ts)
    y_sample = trunk_out(x1s, o_s, yns, Ms).reshape(Bs, Ts, D)
    k_sample = k8s.reshape(Bs, Ts, H, HEAD_DIM)[None]
    v_sample = v8s.reshape(Bs, Ts, H, HEAD_DIM)[None]
    conv_sample = xc[:, -(CONV_WIDTH - 1):][None]

    return (y_prompt, y_sample, k_prompt, v_prompt, conv_prompt, h_p[None],
            k_sample, v_sample, conv_sample, h_s[None])
```
